```python
import jax, jax.numpy as jnp
from jax import lax
import numpy as np

D_MODEL = 1024
BATCH = 16
SEQ = 2048
DEPTH = 4
DEC_BATCH = 8
DEC_SEQ = 16
PAST_LEN = 4096

CHUNK = 64
GMLP_CHUNK = 128
Q_BLOCK = 128
G_A = 4
DG_A = 64
W_A = G_A * DG_A
H_B = 4
NOPE_DIM = 128
ROPE_DIM = 64
V_DIM = 128
W_B = H_B * V_DIM
Q_LORA = 384
KV_LORA = 256
ROPE_THETA = 10000.0
MLA_SCALE = (NOPE_DIM + ROPE_DIM) ** -0.5
H_C = 4
D_C = 64
W_C = H_C * D_C
SB_SCALE = D_C ** -0.5
MIX_WIDTH = W_A + W_B + W_C
W_IN_COLS = 2 * W_A + Q_LORA + KV_LORA + ROPE_DIM + 3 * W_C
D_FF = 4 * D_MODEL
ALPHA = (2 * DEPTH) ** 0.25
BETA = (8 * DEPTH) ** -0.25
EPS = 1e-5

kernel_name = 'hybrid_gmlp_mla_stickbreak_stream_step'


def in_split_points():
    sizes = (W_A, W_A, Q_LORA, KV_LORA, ROPE_DIM, W_C, W_C)
    return [int(v) for v in np.cumsum(sizes)]


def layer_norm(x, g=None, b=None):
    xf = x.astype(jnp.float32)
    mu = xf.mean(-1, keepdims=True)
    var = jnp.square(xf - mu).mean(-1, keepdims=True)
    y = (xf - mu) * lax.rsqrt(var + EPS)
    if g is not None:
        y = y * g.astype(jnp.float32) + b.astype(jnp.float32)
    return y.astype(x.dtype)


def rms_norm(x, g):
    xf = x.astype(jnp.float32)
    y = xf * lax.rsqrt(jnp.mean(jnp.square(xf), -1, keepdims=True) + EPS) * g.astype(jnp.float32)
    return y.astype(x.dtype)


def rope(x, pos):
    half = ROPE_DIM // 2
    inv = ROPE_THETA ** (-jnp.arange(half, dtype=jnp.float32) / half)
    ang = pos.astype(jnp.float32)[:, None] * inv[None, :]
    shape = (1, pos.shape[0]) + (1,) * (x.ndim - 3) + (half,)
    cos = jnp.cos(ang).reshape(shape)
    sin = jnp.sin(ang).reshape(shape)
    xf = x.astype(jnp.float32)
    x1, x2 = xf[..., :half], xf[..., half:]
    return jnp.concatenate([x1 * cos - x2 * sin, x2 * cos + x1 * sin], -1).astype(x.dtype)


def chunk_visible(q_pos, k_pos):
    return (k_pos // CHUNK)[None, :] <= (q_pos // CHUNK)[:, None]


def spatial_gate(u, v, w_s, b_s):
    b, s = u.shape[:2]
    c = min(s, GMLP_CHUNK)
    n_c = s // c
    idx = jnp.arange(c)
    w = jnp.where(chunk_visible(idx, idx)[None], w_s[:, :c, :c], 0)
    vc = v.reshape(b, n_c, c, G_A, DG_A)
    mixed = jnp.einsum('gij,bnjgd->bnigd', w, vc) + b_s[:, :c].T[None, None, :, :, None]
    return u * mixed.reshape(b, s, G_A, DG_A)


def mla_attend(q_pos, q_lat, q_rope, ckv, krope, k_pos):
    scores = (jnp.einsum('bqhc,bkc->bhqk', q_lat, ckv)
              + jnp.einsum('bqhr,bkr->bhqk', q_rope, krope)).astype(jnp.float32) * MLA_SCALE
    scores = jnp.where(chunk_visible(q_pos, k_pos)[None, None], scores, -jnp.inf)
    p = jax.nn.softmax(scores, axis=-1).astype(ckv.dtype)
    return jnp.einsum('bhqk,bkc->bqhc', p, ckv)


def sb_attend(q_pos, q, k, v, k_pos):
    z = jnp.einsum('bqhd,bkhd->bhqk', q, k).astype(jnp.float32) * SB_SCALE
    mask = (k_pos[None, :] < q_pos[:, None])[None, None]
    log_skip = jnp.where(mask, jax.nn.log_sigmoid(-z), 0.0)
    suffix = lax.cumsum(log_skip, axis=3, reverse=True) - log_skip
    w = jnp.where(mask, jnp.exp(jax.nn.log_sigmoid(z) + suffix), 0.0)
    return jnp.einsum('bhqk,bkhd->bqhd', w.astype(v.dtype), v)


def sweep_query_blocks(attend, q_pos, *q_args):
    nq = q_pos.shape[0]
    if nq <= Q_BLOCK:
        return attend(q_pos, *q_args)
    nb = nq // Q_BLOCK
    blocked = tuple(jnp.moveaxis(a.reshape((a.shape[0], nb, Q_BLOCK) + a.shape[2:]), 1, 0) for a in q_args)
    out = lax.map(lambda xs: attend(xs[0], *xs[1]), (q_pos.reshape(nb, Q_BLOCK), blocked))
    out = jnp.moveaxis(out, 0, 1)
    return out.reshape((out.shape[0], nq) + out.shape[3:])


def trunk_layer(x, q_pos, past, w_in, w_s, b_s, g_cq, g_ckv, w_uq, w_uk, w_uv, g_mix, w_out,
                ln1_g, ln1_b, w_up, b_up, w_down, b_down, ln2_g, ln2_b):
    b, s, _ = x.shape
    p = x @ w_in
    a_u, a_v, b_cq, b_ckv, b_kr, c_q, c_k, c_v = jnp.split(p, in_split_points(), axis=-1)

    u = jax.nn.gelu(a_u).reshape(b, s, G_A, DG_A)
    v = layer_norm(jax.nn.gelu(a_v).reshape(b, s, G_A, DG_A))
    y_a = spatial_gate(u, v, w_s, b_s).reshape(b, s, W_A)

    q = jnp.einsum('bsc,chd->bshd', rms_norm(b_cq, g_cq), w_uq)
    q_rope = rope(q[..., NOPE_DIM:], q_pos)
    q_lat = jnp.einsum('bshn,hcn->bshc', q[..., :NOPE_DIM], w_uk)
    ckv = rms_norm(b_ckv, g_ckv)
    krope = rope(b_kr, q_pos)

    qc = c_q.reshape(b, s, H_C, D_C)
    kc = c_k.reshape(b, s, H_C, D_C)
    vc = c_v.reshape(b, s, H_C, D_C)

    if past is None:
        ckv_all, kr_all, k_all, v_all, k_pos = ckv, krope, kc, vc, q_pos
    else:
        ckv_p, kr_p, k_p, v_p = past
        ckv_all = jnp.concatenate([ckv_p, ckv], 1)
        kr_all = jnp.concatenate([kr_p, krope], 1)
        k_all = jnp.concatenate([k_p, kc], 1)
        v_all = jnp.concatenate([v_p, vc], 1)
        k_pos = jnp.concatenate([jnp.arange(ckv_p.shape[1], dtype=jnp.int32), q_pos])

    out_lat = sweep_query_blocks(lambda qp, ql, qr: mla_attend(qp, ql, qr, ckv_all, kr_all, k_pos),
                                 q_pos, q_lat, q_rope)
    y_b = jnp.einsum('bshc,hcd->bshd', out_lat, w_uv).reshape(b, s, W_B)
    y_c = sweep_query_blocks(lambda qp, qq: sb_attend(qp, qq, k_all, v_all, k_pos),
                             q_pos, qc).reshape(b, s, W_C)

    g_a, g_b, g_c = jnp.split(g_mix, [W_A, W_A + W_B])
    y = jnp.concatenate([rms_norm(y_a, g_a), rms_norm(y_b, g_b), rms_norm(y_c, g_c)], -1)
    x = layer_norm(ALPHA * x + y @ w_out, ln1_g, ln1_b)

    h = jnp.square(jax.nn.relu(x @ w_up + b_up)) @ w_down + b_down
    x = layer_norm(ALPHA * x + h, ln2_g, ln2_b)
    return x, (ckv, krope, kc, vc, v)


def setup_inputs(seed: int = 0) -> dict:
    key = jax.random.key(seed)
    ks = jax.random.split(key, 26)

    def nrm(k, shape, scale=1.0):
        return jax.random.normal(k, shape, jnp.float32) * scale

    return {
        'x_prompt': nrm(ks[0], (BATCH, SEQ, D_MODEL)),
        'x_sample': nrm(ks[1], (DEC_BATCH, DEC_SEQ, D_MODEL)),
        'cache_mla_ckv': nrm(ks[2], (DEPTH, DEC_BATCH, PAST_LEN, KV_LORA)),
        'cache_mla_krope': nrm(ks[3], (DEPTH, DEC_BATCH, PAST_LEN, ROPE_DIM)),
        'cache_sb_k': nrm(ks[4], (DEPTH, DEC_BATCH, PAST_LEN, H_C, D_C)),
        'cache_sb_v': nrm(ks[5], (DEPTH, DEC_BATCH, PAST_LEN, H_C, D_C)),
        'w_in': nrm(ks[6], (DEPTH, D_MODEL, W_IN_COLS), D_MODEL ** -0.5),
        'w_s': nrm(ks[7], (DEPTH, G_A, GMLP_CHUNK, GMLP_CHUNK), GMLP_CHUNK ** -0.5),
        'b_s': 1.0 + nrm(ks[8], (DEPTH, G_A, GMLP_CHUNK), 0.1),
        'g_cq': 1.0 + nrm(ks[9], (DEPTH, Q_LORA), 0.1),
        'g_ckv': 1.0 + nrm(ks[10], (DEPTH, KV_LORA), 0.1),
        'w_uq': nrm(ks[11], (DEPTH, Q_LORA, H_B, NOPE_DIM + ROPE_DIM), Q_LORA ** -0.5),
        'w_uk': nrm(ks[12], (DEPTH, H_B, KV_LORA, NOPE_DIM), KV_LORA ** -0.5),
        'w_uv': nrm(ks[13], (DEPTH, H_B, KV_LORA, V_DIM), KV_LORA ** -0.5),
        'g_mix': 1.0 + nrm(ks[14], (DEPTH, MIX_WIDTH), 0.1),
        'w_out': nrm(ks[15], (DEPTH, MIX_WIDTH, D_MODEL), MIX_WIDTH ** -0.5 * BETA),
        'ln1_g': 1.0 + nrm(ks[16], (DEPTH, D_MODEL), 0.1),
        'ln1_b': nrm(ks[17], (DEPTH, D_MODEL), 0.02),
        'w_up': nrm(ks[18], (DEPTH, D_MODEL, D_FF), D_MODEL ** -0.5 * BETA),
        'b_up': nrm(ks[19], (DEPTH, D_FF), 0.02),
        'w_down': nrm(ks[20], (DEPTH, D_FF, D_MODEL), D_FF ** -0.5 * BETA),
        'b_down': nrm(ks[21], (DEPTH, D_MODEL), 0.02),
        'ln2_g': 1.0 + nrm(ks[22], (DEPTH, D_MODEL), 0.1),
        'ln2_b': nrm(ks[23], (DEPTH, D_MODEL), 0.02),
    }


def reference(x_prompt, x_sample, cache_mla_ckv, cache_mla_krope, cache_sb_k, cache_sb_v,
              w_in, w_s, b_s, g_cq, g_ckv, w_uq, w_uk, w_uv, g_mix, w_out,
              ln1_g, ln1_b, w_up, b_up, w_down, b_down, ln2_g, ln2_b):
    past_len = cache_mla_ckv.shape[2]
    pos_p = jnp.arange(x_prompt.shape[1], dtype=jnp.int32)
    pos_s = past_len + jnp.arange(x_sample.shape[1], dtype=jnp.int32)
    yp, ys = x_prompt, x_sample
    ckv_p, kr_p, k_p, v_p = [], [], [], []
    ckv_s, kr_s, k_s, v_s, gv_s = [], [], [], [], []
    for l in range(DEPTH):
        params = (w_in[l], w_s[l], b_s[l], g_cq[l], g_ckv[l], w_uq[l], w_uk[l], w_uv[l], g_mix[l],
                  w_out[l], ln1_g[l], ln1_b[l], w_up[l], b_up[l], w_down[l], b_down[l],
                  ln2_g[l], ln2_b[l])
        yp, st_p = trunk_layer(yp, pos_p, None, *params)
        ys, st_s = trunk_layer(ys, pos_s, (cache_mla_ckv[l], cache_mla_krope[l],
                                           cache_sb_k[l], cache_sb_v[l]), *params)
        ckv_p.append(st_p[0]); kr_p.append(st_p[1]); k_p.append(st_p[2]); v_p.append(st_p[3])
        ckv_s.append(st_s[0]); kr_s.append(st_s[1]); k_s.append(st_s[2]); v_s.append(st_s[3])
        gv_s.append(st_s[4])
    return (yp, ys,
            jnp.stack(ckv_p), jnp.stack(kr_p), jnp.stack(k_p), jnp.stack(v_p),
            jnp.stack(ckv_s), jnp.stack(kr_s), jnp.stack(k_s), jnp.stack(v_s), jnp.stack(gv_s))
```

```python
import functools

import jax
import jax.numpy as jnp
from jax import lax
from jax.experimental import pallas as pl
from jax.experimental.pallas import tpu as pltpu

F32 = jnp.float32
BF16 = jnp.bfloat16

CHUNK = 64
GMLP_CHUNK = 128
G_A, DG_A = 4, 64
W_A = G_A * DG_A
H_B, NOPE_DIM, ROPE_DIM, V_DIM = 4, 128, 64, 128
W_B = H_B * V_DIM
Q_LORA, KV_LORA = 384, 256
ROPE_THETA = 10000.0
MLA_SCALE = (NOPE_DIM + ROPE_DIM) ** -0.5
H_C, D_C = 4, 64
W_C = H_C * D_C
SB_SCALE = D_C ** -0.5
EPS = 1e-5
NEG_BIG = -1e30

LANES = 128
VMEM_LIMIT_BYTES = 56 * 1024 * 1024


def _dot(a, b):
    return jnp.dot(a, b, preferred_element_type=F32)


def _dot_nt(a, b):
    return lax.dot_general(a, b, (((1,), (1,)), ((), ())), preferred_element_type=F32)


def _rep(x, n):
    return x if n == 1 else jnp.concatenate([x] * n, axis=1)


def _rms(x, g):
    ms = jnp.mean(x * x, axis=-1, keepdims=True)
    return x * lax.rsqrt(ms + EPS) * g


def _layer_norm(x, g, b):
    mu = jnp.mean(x, axis=-1, keepdims=True)
    d = x - mu
    var = jnp.mean(d * d, axis=-1, keepdims=True)
    return d * lax.rsqrt(var + EPS) * g + b


def _gelu_tanh(x):
    c = 0.7978845608028654
    return 0.5 * x * (1.0 + jnp.tanh(c * (x + 0.044715 * (x * x * x))))


def _head_lane_id(width, per_head):
    return lax.broadcasted_iota(jnp.int32, (1, width), 1) // per_head


def _in_proj_kernel(x_ref, wa_ref, wcq_ref, wckv_ref, wkr_ref, wc_ref, gcq_ref, gckv_ref,
                    wuqn_ref, wuqr_ref, wukt_ref, mixw_ref, mixb_ref, ga_ref, seg_ref,
                    cos_ref, sin_ref,
                    ya_ref, ql_ref, qr_ref, kcat_ref, sbq_ref, sbk_ref, sbv_ref,
                    ckv_ref, kr_ref, k_ref, v_ref, *gv_refs, tm):
    xb = x_ref[...].astype(BF16)
    grp = _head_lane_id(W_A, DG_A)
    lower_half = (lax.broadcasted_iota(jnp.int32, (1, 256), 1) % ROPE_DIM) < (ROPE_DIM // 2)
    cos = cos_ref[...]
    sin = sin_ref[...]

    def seg_mean(val):
        hi = val.astype(BF16)
        lo = (val - hi.astype(F32)).astype(BF16)
        return (_dot(hi, seg_ref[...]) + _dot(lo, seg_ref[...])) * (1.0 / DG_A)

    def rope(val):
        swapped = jnp.where(lower_half, pltpu.roll(val, 256 - ROPE_DIM // 2, 1),
                            pltpu.roll(val, ROPE_DIM // 2, 1))
        return val * cos + swapped * sin

    pa = _dot(xb, wa_ref[...])
    u = _gelu_tanh(pa[:, :W_A])
    gv = _gelu_tanh(pa[:, W_A:])
    mu = seg_mean(gv)
    dv = gv - mu
    v = dv * lax.rsqrt(seg_mean(dv * dv) + EPS)
    if gv_refs:
        gv_refs[0][...] = v
    vb = v.astype(BF16)
    zero_b = jnp.zeros((GMLP_CHUNK, W_A), BF16)
    mixed = []
    for c in range(tm // GMLP_CHUNK):
        vc = vb[c * GMLP_CHUNK:(c + 1) * GMLP_CHUNK, :]
        stacked = jnp.concatenate([jnp.where(grp == g, vc, zero_b) for g in range(G_A)], axis=0)
        mixed.append(_dot(mixw_ref[...], stacked) + mixb_ref[...])
    y_a = u * jnp.concatenate(mixed, axis=0)
    ya_ref[...] = _rms(y_a, ga_ref[...]).astype(BF16)

    cqn = _rms(_dot(xb, wcq_ref[...]), gcq_ref[...]).astype(BF16)
    qn = _dot(cqn, wuqn_ref[...])
    qr = rope(_dot(cqn, wuqr_ref[...]))
    qr_ref[...] = (qr * MLA_SCALE).astype(BF16)
    for h in range(H_B):
        ql = _dot(qn[:, h * NOPE_DIM:(h + 1) * NOPE_DIM].astype(BF16), wukt_ref[h])
        ql_ref[:, h * KV_LORA:(h + 1) * KV_LORA] = (ql * MLA_SCALE).astype(BF16)
    ckv = _rms(_dot(xb, wckv_ref[...]), gckv_ref[...])
    ckv_ref[...] = ckv
    kr4 = rope(_dot(xb, wkr_ref[...]))
    kr_ref[...] = kr4[:, :ROPE_DIM]
    kcat_ref[:, :KV_LORA] = ckv.astype(BF16)
    kcat_ref[:, KV_LORA:] = kr4.astype(BF16)

    pc = _dot(xb, wc_ref[...])
    sbq_ref[...] = (pc[:, :W_C] * SB_SCALE).astype(BF16)
    kc = pc[:, W_C:2 * W_C]
    vc_ = pc[:, 2 * W_C:]
    k_ref[...] = kc
    v_ref[...] = vc_
    sbk_ref[...] = kc.astype(BF16)
    sbv_ref[...] = vc_.astype(BF16)


def _in_proj(x2d, lw, cos_tab, sin_tab, *, tm, want_gv):
    t = x2d.shape[0]
    n_pos_blocks = cos_tab.shape[0] // tm
    tok = lambda w: pl.BlockSpec((tm, w), lambda i: (i, 0))
    full = lambda a: pl.BlockSpec(a.shape, lambda i: (0,) * a.ndim)
    pos = pl.BlockSpec((tm, 256), lambda i: (i % n_pos_blocks, 0))
    weights = (lw['w_a'], lw['w_cq'], lw['w_ckv'], lw['w_kr4'], lw['w_c'], lw['g_cq'], lw['g_ckv'],
               lw['w_uqn'], lw['w_uqr'], lw['w_ukt'], lw['mixw'], lw['mixb'], lw['g_a'], lw['seg'])
    out_shape = [
        jax.ShapeDtypeStruct((t, W_A), BF16),
        jax.ShapeDtypeStruct((t, H_B * KV_LORA), BF16),
        jax.ShapeDtypeStruct((t, H_B * ROPE_DIM), BF16),
        jax.ShapeDtypeStruct((t, 2 * KV_LORA), BF16),
        jax.ShapeDtypeStruct((t, W_C), BF16),
        jax.ShapeDtypeStruct((t, W_C), BF16),
        jax.ShapeDtypeStruct((t, W_C), BF16),
        jax.ShapeDtypeStruct((t, KV_LORA), F32),
        jax.ShapeDtypeStruct((t, ROPE_DIM), F32),
        jax.ShapeDtypeStruct((t, W_C), F32),
        jax.ShapeDtypeStruct((t, W_C), F32),
    ]
    if want_gv:
        out_shape.append(jax.ShapeDtypeStruct((t, W_A), F32))
    return pl.pallas_call(
        functools.partial(_in_proj_kernel, tm=tm),
        grid=(t // tm,),
        in_specs=[tok(x2d.shape[1])] + [full(w) for w in weights] + [pos, pos],
        out_specs=[tok(s.shape[1]) for s in out_shape],
        out_shape=out_shape,
        compiler_params=pltpu.CompilerParams(dimension_semantics=("parallel",),
                                             vmem_limit_bytes=VMEM_LIMIT_BYTES),
        name="in_proj",
    )(x2d, *weights, cos_tab, sin_tab)


def _mla_finish(acc, l_inv, wuv_ref, gb_ref, rows):
    out = acc * _rep(l_inv, KV_LORA // LANES)
    ys = [_dot(out[h * rows:(h + 1) * rows].astype(BF16), wuv_ref[h]) for h in range(H_B)]
    return _rms(jnp.concatenate(ys, axis=1), gb_ref[...]).astype(BF16)


def _mla_kernel(ql_ref, qr_ref, kcat_ref, wuv_ref, gb_ref, o_ref, qs_ref, m_ref, l_ref, acc_ref,
                *, tq):
    qi = pl.program_id(1)
    head = _head_lane_id(H_B * ROPE_DIM, ROPE_DIM)
    qr = qr_ref[0]
    for h in range(H_B):
        qs_ref[h * tq:(h + 1) * tq, :KV_LORA] = ql_ref[0, :, h * KV_LORA:(h + 1) * KV_LORA]
        qs_ref[h * tq:(h + 1) * tq, KV_LORA:] = jnp.where(head == h, qr, jnp.zeros_like(qr))
    m_ref[...] = jnp.full(m_ref.shape, NEG_BIG, F32)
    l_ref[...] = jnp.zeros(l_ref.shape, F32)
    acc_ref[...] = jnp.zeros(acc_ref.shape, F32)

    def step(j, masked):
        kb = kcat_ref[0, pl.ds(pl.multiple_of(j * tq, tq), tq), :]
        s = _dot_nt(qs_ref[...], kb)
        if masked:
            row = lax.broadcasted_iota(jnp.int32, s.shape, 0) % tq
            col = lax.broadcasted_iota(jnp.int32, s.shape, 1)
            s = jnp.where((col // CHUNK) <= (row // CHUNK), s, NEG_BIG)
        m_prev = m_ref[...]
        m_next = jnp.maximum(m_prev, jnp.max(s, axis=1, keepdims=True))
        p = jnp.exp(s - _rep(m_next, tq // LANES))
        alpha = jnp.exp(m_prev - m_next)
        l_ref[...] = alpha * l_ref[...] + jnp.sum(p, axis=1, keepdims=True)
        m_ref[...] = m_next
        acc_ref[...] = (acc_ref[...] * _rep(alpha, KV_LORA // LANES)
                        + _dot(p.astype(BF16), kb[:, :KV_LORA]))

    def body(j, carry):
        step(j, False)
        return carry

    lax.fori_loop(0, qi, body, 0)
    step(qi, True)
    o_ref[0] = _mla_finish(acc_ref[...], 1.0 / l_ref[...], wuv_ref, gb_ref, tq)


def _mla_prompt(ql, qr, kcat, w_uv, g_b, *, tq):
    b, s, _ = ql.shape
    rows = H_B * tq
    return pl.pallas_call(
        functools.partial(_mla_kernel, tq=tq),
        grid=(b, s // tq),
        in_specs=[
            pl.BlockSpec((1, tq, H_B * KV_LORA), lambda bi, qi: (bi, qi, 0)),
            pl.BlockSpec((1, tq, H_B * ROPE_DIM), lambda bi, qi: (bi, qi, 0)),
            pl.BlockSpec((1, s, 2 * KV_LORA), lambda bi, qi: (bi, 0, 0)),
            pl.BlockSpec(w_uv.shape, lambda bi, qi: (0, 0, 0)),
            pl.BlockSpec(g_b.shape, lambda bi, qi: (0, 0)),
        ],
        out_specs=pl.BlockSpec((1, tq, W_B), lambda bi, qi: (bi, qi, 0)),
        out_shape=jax.ShapeDtypeStruct((b, s, W_B), BF16),
        scratch_shapes=[pltpu.VMEM((rows, 2 * KV_LORA), BF16),
                        pltpu.VMEM((rows, LANES), F32),
                        pltpu.VMEM((rows, LANES), F32),
                        pltpu.VMEM((rows, KV_LORA), F32)],
        compiler_params=pltpu.CompilerParams(dimension_semantics=("parallel", "arbitrary"),
                                             vmem_limit_bytes=VMEM_LIMIT_BYTES),
        name="mla_prompt",
    )(ql, qr, kcat, w_uv, g_b)


def _sb_block(qs, kb, vb, tri, carry, vis):
    z = _dot_nt(qs, kb)
    t = jnp.log1p(jnp.exp(-jnp.abs(z)))
    log_skip = -jnp.maximum(z, 0.0) - t
    log_take = jnp.minimum(z, 0.0) - t
    if vis is not None:
        log_skip = jnp.where(vis, log_skip, 0.0)
    hi = log_skip.astype(BF16)
    lo = (log_skip - hi.astype(F32)).astype(BF16)
    n = z.shape[1] // LANES
    suffix = _dot(hi, tri) + _dot(lo, tri) + _rep(carry, n)
    w = jnp.exp(log_take + suffix)
    if vis is not None:
        w = jnp.where(vis, w, 0.0)
    return _dot(w.astype(BF16), vb), carry + jnp.sum(log_skip, axis=1, keepdims=True)


def _stack_heads_masked(q, per_head):
    head = _head_lane_id(q.shape[1], per_head)
    return jnp.concatenate([jnp.where(head == h, q, jnp.zeros_like(q)) for h in range(H_C)], axis=0)


def _unstack_heads(acc, rows, per_head):
    head = _head_lane_id(acc.shape[1], per_head)
    y = jnp.zeros((rows, acc.shape[1]), F32)
    for h in range(H_C):
        y = y + jnp.where(head == h, acc[h * rows:(h + 1) * rows], 0.0)
    return y


def _sb_kernel(q_ref, k_ref, v_ref, tri_ref, gc_ref, o_ref, qs_ref, c_ref, acc_ref, *, tq):
    qi = pl.program_id(1)
    qs_ref[...] = _stack_heads_masked(q_ref[0], D_C)
    c_ref[...] = jnp.zeros(c_ref.shape, F32)
    acc_ref[...] = jnp.zeros(acc_ref.shape, F32)

    def step(j, masked):
        off = pl.multiple_of(j * tq, tq)
        vis = None
        if masked:
            shape = (H_C * tq, tq)
            row = lax.broadcasted_iota(jnp.int32, shape, 0) % tq
            col = lax.broadcasted_iota(jnp.int32, shape, 1)
            vis = col < row
        pv, carry = _sb_block(qs_ref[...], k_ref[0, pl.ds(off, tq), :], v_ref[0, pl.ds(off, tq), :],
                              tri_ref[...], c_ref[...], vis)
        acc_ref[...] += pv
        c_ref[...] = carry

    step(qi, True)

    def body(jj, carry):
        step(qi - 1 - jj, False)
        return carry

    lax.fori_loop(0, qi, body, 0)
    o_ref[0] = _rms(_unstack_heads(acc_ref[...], tq, D_C), gc_ref[...]).astype(BF16)


def _sb_prompt(q, k, v, tri, g_c, *, tq):
    b, s, _ = q.shape
    rows = H_C * tq
    return pl.pallas_call(
        functools.partial(_sb_kernel, tq=tq),
        grid=(b, s // tq),
        in_specs=[
            pl.BlockSpec((1, tq, W_C), lambda bi, qi: (bi, qi, 0)),
            pl.BlockSpec((1, s, W_C), lambda bi, qi: (bi, 0, 0)),
            pl.BlockSpec((1, s, W_C), lambda bi, qi: (bi, 0, 0)),
            pl.BlockSpec(tri.shape, lambda bi, qi: (0, 0)),
            pl.BlockSpec(g_c.shape, lambda bi, qi: (0, 0)),
        ],
        out_specs=pl.BlockSpec((1, tq, W_C), lambda bi, qi: (bi, qi, 0)),
        out_shape=jax.ShapeDtypeStruct((b, s, W_C), BF16),
        scratch_shapes=[pltpu.VMEM((rows, W_C), BF16),
                        pltpu.VMEM((rows, LANES), F32),
                        pltpu.VMEM((rows, W_C), F32)],
        compiler_params=pltpu.CompilerParams(dimension_semantics=("parallel", "arbitrary"),
                                             vmem_limit_bytes=VMEM_LIMIT_BYTES),
        name="sb_prompt",
    )(q, k, v, tri, g_c)


def _mla_dec_kernel(ql_ref, qr_ref, ckv_p_ref, kr_p_ref, ckv_n_ref, kr_n_ref, wuv_ref, gb_ref,
                    o_ref, m_ref, l_ref, acc_ref, *, n_new):
    kb_i = pl.program_id(1)

    @pl.when(kb_i == 0)
    def _():
        m_ref[...] = jnp.full(m_ref.shape, NEG_BIG, F32)
        l_ref[...] = jnp.zeros(l_ref.shape, F32)
        acc_ref[...] = jnp.zeros(acc_ref.shape, F32)

    def update(ckv_f32, kr_f32, n_valid):
        ckv_b = ckv_f32.astype(BF16)
        s = _dot_nt(ql_ref[0], ckv_b) + _dot_nt(qr_ref[0], kr_f32.astype(BF16))
        if n_valid is not None:
            s = jnp.where(lax.broadcasted_iota(jnp.int32, s.shape, 1) < n_valid, s, NEG_BIG)
        m_prev = m_ref[...]
        m_next = jnp.maximum(m_prev, jnp.max(s, axis=1, keepdims=True))
        p = jnp.exp(s - _rep(m_next, s.shape[1] // LANES))
        alpha = jnp.exp(m_prev - m_next)
        l_ref[...] = alpha * l_ref[...] + jnp.sum(p, axis=1, keepdims=True)
        m_ref[...] = m_next
        acc_ref[...] = acc_ref[...] * _rep(alpha, KV_LORA // LANES) + _dot(p.astype(BF16), ckv_b)

    update(ckv_p_ref[...], kr_p_ref[...], None)

    @pl.when(kb_i == pl.num_programs(1) - 1)
    def _():
        update(ckv_n_ref[0], kr_n_ref[0], n_new)
        o_ref[0] = _mla_finish(acc_ref[...], 1.0 / l_ref[...], wuv_ref, gb_ref, n_new)


def _mla_decode(ql_s, qr_s, cache_ckv, cache_kr, layer, ckv_new, kr_new, w_uv, g_b, *, tk):
    b, rows, _ = ql_s.shape
    n_new = rows // H_B
    past = cache_ckv.shape[2]
    return pl.pallas_call(
        functools.partial(_mla_dec_kernel, n_new=n_new),
        grid=(b, past // tk),
        in_specs=[
            pl.BlockSpec((1, rows, KV_LORA), lambda bi, ki: (bi, 0, 0)),
            pl.BlockSpec((1, rows, ROPE_DIM), lambda bi, ki: (bi, 0, 0)),
            pl.BlockSpec((None, None, tk, KV_LORA), lambda bi, ki: (layer, bi, ki, 0)),
            pl.BlockSpec((None, None, tk, ROPE_DIM), lambda bi, ki: (layer, bi, ki, 0)),
            pl.BlockSpec((1,) + ckv_new.shape[1:], lambda bi, ki: (bi, 0, 0)),
            pl.BlockSpec((1,) + kr_new.shape[1:], lambda bi, ki: (bi, 0, 0)),
            pl.BlockSpec(w_uv.shape, lambda bi, ki: (0, 0, 0)),
            pl.BlockSpec(g_b.shape, lambda bi, ki: (0, 0)),
        ],
        out_specs=pl.BlockSpec((1, n_new, W_B), lambda bi, ki: (bi, 0, 0)),
        out_shape=jax.ShapeDtypeStruct((b, n_new, W_B), BF16),
        scratch_shapes=[pltpu.VMEM((rows, LANES), F32),
                        pltpu.VMEM((rows, LANES), F32),
                        pltpu.VMEM((rows, KV_LORA), F32)],
        compiler_params=pltpu.CompilerParams(dimension_semantics=("parallel", "arbitrary"),
                                             vmem_limit_bytes=VMEM_LIMIT_BYTES),
        name="mla_decode",
    )(ql_s, qr_s, cache_ckv, cache_kr, ckv_new, kr_new, w_uv, g_b)


def _sb_dec_kernel(q_ref, k_p_ref, v_p_ref, k_n_ref, v_n_ref, tri_p_ref, tri_n_ref, gc_ref,
                   o_ref, qs_ref, c_ref, acc_ref, *, n_new):
    kb_i = pl.program_id(1)

    @pl.when(kb_i == 0)
    def _():
        qs_ref[...] = _stack_heads_masked(q_ref[...], D_C)
        shape = (H_C * n_new, k_n_ref.shape[1])
        row = lax.broadcasted_iota(jnp.int32, shape, 0) % n_new
        col = lax.broadcasted_iota(jnp.int32, shape, 1)
        pv, carry = _sb_block(qs_ref[...], k_n_ref[0], v_n_ref[0], tri_n_ref[...],
                              jnp.zeros(c_ref.shape, F32), col < row)
        acc_ref[...] = pv
        c_ref[...] = carry

    pv, carry = _sb_block(qs_ref[...], k_p_ref[...].astype(BF16), v_p_ref[...].astype(BF16),
                          tri_p_ref[...], c_ref[...], None)
    acc_ref[...] += pv
    c_ref[...] = carry

    @pl.when(kb_i == pl.num_programs(1) - 1)
    def _():
        o_ref[...] = _rms(_unstack_heads(acc_ref[...], n_new, D_C), gc_ref[...]).astype(BF16)


def _sb_decode(q, cache_k, cache_v, layer, k_new, v_new, tri_p, tri_n, g_c, *, n_new, tk):
    t = q.shape[0]
    b = t // n_new
    nkb = cache_k.shape[2] // tk
    rows = H_C * n_new
    return pl.pallas_call(
        functools.partial(_sb_dec_kernel, n_new=n_new),
        grid=(b, nkb),
        in_specs=[
            pl.BlockSpec((n_new, W_C), lambda bi, ki: (bi, 0)),
            pl.BlockSpec((None, None, tk, W_C), lambda bi, ki: (layer, bi, nkb - 1 - ki, 0)),
            pl.BlockSpec((None, None, tk, W_C), lambda bi, ki: (layer, bi, nkb - 1 - ki, 0)),
            pl.BlockSpec((1,) + k_new.shape[1:], lambda bi, ki: (bi, 0, 0)),
            pl.BlockSpec((1,) + v_new.shape[1:], lambda bi, ki: (bi, 0, 0)),
            pl.BlockSpec(tri_p.shape, lambda bi, ki: (0, 0)),
            pl.BlockSpec(tri_n.shape, lambda bi, ki: (0, 0)),
            pl.BlockSpec(g_c.shape, lambda bi, ki: (0, 0)),
        ],
        out_specs=pl.BlockSpec((n_new, W_C), lambda bi, ki: (bi, 0)),
        out_shape=jax.ShapeDtypeStruct((t, W_C), BF16),
        scratch_shapes=[pltpu.VMEM((rows, W_C), BF16),
                        pltpu.VMEM((rows, LANES), F32),
                        pltpu.VMEM((rows, W_C), F32)],
        compiler_params=pltpu.CompilerParams(dimension_semantics=("parallel", "arbitrary"),
                                             vmem_limit_bytes=VMEM_LIMIT_BYTES),
        name="sb_decode",
    )(q, cache_k, cache_v, k_new, v_new, tri_p, tri_n, g_c)


def _out_mlp_kernel(x_ref, ya_ref, yb_ref, yc_ref, wout_ref, g1_ref, b1_ref, wup_ref, bup_ref,
                    wdown_ref, bdown_ref, g2_ref, b2_ref, o_ref, *, alpha, ff_chunk):
    y = jnp.concatenate([ya_ref[...], yb_ref[...], yc_ref[...]], axis=1)
    x1 = _layer_norm(alpha * x_ref[...] + _dot(y, wout_ref[...]), g1_ref[...], b1_ref[...])
    x1b = x1.astype(BF16)
    h = jnp.zeros(x1.shape, F32)
    for c in range(wup_ref.shape[1] // ff_chunk):
        sl = slice(c * ff_chunk, (c + 1) * ff_chunk)
        a = jnp.maximum(_dot(x1b, wup_ref[:, sl]) + bup_ref[:, sl], 0.0)
        h = h + _dot((a * a).astype(BF16), wdown_ref[sl, :])
    o_ref[...] = _layer_norm(alpha * x1 + h + bdown_ref[...], g2_ref[...], b2_ref[...])


def _out_mlp(x2d, ya, yb, yc, lw, *, tm, alpha):
    t, d = x2d.shape
    tok = lambda w: pl.BlockSpec((tm, w), lambda i: (i, 0))
    const = lambda a: pl.BlockSpec(a.shape, lambda i: (0,) * a.ndim, pipeline_mode=pl.Buffered(1))
    weights = (lw['w_out'], lw['ln1_g'], lw['ln1_b'], lw['w_up'], lw['b_up'], lw['w_down'],
               lw['b_down'], lw['ln2_g'], lw['ln2_b'])
    return pl.pallas_call(
        functools.partial(_out_mlp_kernel, alpha=alpha, ff_chunk=1024),
        grid=(t // tm,),
        in_specs=[tok(d), tok(W_A), tok(W_B), tok(W_C)] + [const(w) for w in weights],
        out_specs=tok(d),
        out_shape=jax.ShapeDtypeStruct((t, d), F32),
        compiler_params=pltpu.CompilerParams(dimension_semantics=("parallel",),
                                             vmem_limit_bytes=VMEM_LIMIT_BYTES),
        name="out_mlp",
    )(x2d, ya, yb, yc, *weights)


def _rope_tables(pos):
    half = ROPE_DIM // 2
    inv = ROPE_THETA ** (-jnp.arange(half, dtype=F32) / half)
    ang = pos.astype(F32)[:, None] * inv[None, :]
    cos, sin = jnp.cos(ang), jnp.sin(ang)
    cos_t = jnp.tile(jnp.concatenate([cos, cos], -1), (1, H_B))
    sin_t = jnp.tile(jnp.concatenate([-sin, sin], -1), (1, H_B))
    return cos_t, sin_t


def _strict_lower_ones(n):
    j = lax.broadcasted_iota(jnp.int32, (n, n), 0)
    s = lax.broadcasted_iota(jnp.int32, (n, n), 1)
    return (j > s).astype(BF16)


def _row(a):
    return a.reshape(1, -1).astype(F32)


def _prep_layer(l, w_in, w_s, b_s, g_cq, g_ckv, w_uq, w_uk, w_uv, g_mix, w_out, ln1_g, ln1_b,
                w_up, b_up, w_down, b_down, ln2_g, ln2_b, n_dec):
    wi = w_in[l].astype(BF16)
    o_cq, o_ckv, o_kr, o_c = 2 * W_A, 2 * W_A + Q_LORA, 2 * W_A + Q_LORA + KV_LORA, \
        2 * W_A + Q_LORA + KV_LORA + ROPE_DIM
    uq = w_uq[l].astype(BF16)
    lw = {
        'w_a': wi[:, :o_cq], 'w_cq': wi[:, o_cq:o_ckv], 'w_ckv': wi[:, o_ckv:o_kr],
        'w_kr4': jnp.tile(wi[:, o_kr:o_c], (1, H_B)), 'w_c': wi[:, o_c:],
        'g_cq': _row(g_cq[l]), 'g_ckv': _row(g_ckv[l]),
        'w_uqn': uq[:, :, :NOPE_DIM].reshape(Q_LORA, H_B * NOPE_DIM),
        'w_uqr': uq[:, :, NOPE_DIM:].reshape(Q_LORA, H_B * ROPE_DIM),
        'w_ukt': jnp.swapaxes(w_uk[l], 1, 2).astype(BF16),
        'w_uv': w_uv[l].astype(BF16),
        'g_a': _row(g_mix[l, :W_A]), 'g_b': _row(g_mix[l, W_A:W_A + W_B]),
        'g_c': _row(g_mix[l, W_A + W_B:]),
        'w_out': w_out[l].astype(BF16), 'ln1_g': _row(ln1_g[l]), 'ln1_b': _row(ln1_b[l]),
        'w_up': w_up[l].astype(BF16), 'b_up': _row(b_up[l]), 'w_down': w_down[l].astype(BF16),
        'b_down': _row(b_down[l]), 'ln2_g': _row(ln2_g[l]), 'ln2_b': _row(ln2_b[l]),
    }
    seg = lax.broadcasted_iota(jnp.int32, (W_A, W_A), 0) // DG_A
    lw['seg'] = (seg == seg.T).astype(BF16)
    idx = jnp.arange(GMLP_CHUNK)
    vis = (idx[None, :] // CHUNK) <= (idx[:, None] // CHUNK)
    wm = jnp.where(vis[None], w_s[l], 0.0)
    lw['mixw'] = jnp.transpose(wm, (1, 0, 2)).reshape(GMLP_CHUNK, G_A * GMLP_CHUNK).astype(BF16)
    lw['mixb'] = jnp.repeat(b_s[l].T, DG_A, axis=1).astype(F32)
    reps = GMLP_CHUNK // n_dec
    wd = w_s[l][:, :n_dec, :n_dec]
    wmd = jax.vmap(lambda w: jnp.kron(jnp.eye(reps, dtype=F32), w))(wd)
    lw_dec = dict(lw)
    lw_dec['mixw'] = jnp.transpose(wmd, (1, 0, 2)).reshape(GMLP_CHUNK, G_A * GMLP_CHUNK).astype(BF16)
    lw_dec['mixb'] = jnp.repeat(jnp.tile(b_s[l][:, :n_dec].T, (reps, 1)), DG_A, axis=1).astype(F32)
    return lw, lw_dec


def kernel(x_prompt, x_sample, cache_mla_ckv, cache_mla_krope, cache_sb_k, cache_sb_v, w_in, w_s, b_s, g_cq, g_ckv, w_uq, w_uk, w_uv, g_mix, w_out, ln1_g, ln1_b, w_up, b_up, w_down, b_down, ln2_g, ln2_b):
    b, s, d = x_prompt.shape
    db, ds, _ = x_sample.shape
    depth = w_in.shape[0]
    past = cache_mla_ckv.shape[2]
    alpha = (2 * depth) ** 0.25
    tm_p, tq = 512, 256
    t_dec = db * ds
    assert t_dec == GMLP_CHUNK and s % tm_p == 0 and past % CHUNK == 0 and ds <= CHUNK

    cos_p, sin_p = _rope_tables(jnp.arange(s, dtype=jnp.int32))
    cos_s, sin_s = _rope_tables(jnp.tile(past + jnp.arange(ds, dtype=jnp.int32), db))
    tri_p = _strict_lower_ones(tq)
    tk_dec = 1024
    tri_dp = _strict_lower_ones(tk_dec)
    tri_dn = _strict_lower_ones(LANES)
    cache_k2 = cache_sb_k.reshape(depth, db, past, W_C)
    cache_v2 = cache_sb_v.reshape(depth, db, past, W_C)

    xp = x_prompt.reshape(b * s, d)
    xs = x_sample.reshape(t_dec, d)
    st_p = [[] for _ in range(4)]
    st_s = [[] for _ in range(5)]
    for l in range(depth):
        lw, lw_dec = _prep_layer(l, w_in, w_s, b_s, g_cq, g_ckv, w_uq, w_uk, w_uv, g_mix, w_out,
                                 ln1_g, ln1_b, w_up, b_up, w_down, b_down, ln2_g, ln2_b, ds)
        ya, ql, qr, kcat, sbq, sbk, sbv, ckv, kr, kk, vv = _in_proj(
            xp, lw, cos_p, sin_p, tm=tm_p, want_gv=False)
        r3 = lambda a: a.reshape(b, s, a.shape[-1])
        yb = _mla_prompt(r3(ql), r3(qr), r3(kcat), lw['w_uv'], lw['g_b'], tq=tq)
        yc = _sb_prompt(r3(sbq), r3(sbk), r3(sbv), tri_p, lw['g_c'], tq=tq)
        xp = _out_mlp(xp, ya, yb.reshape(b * s, W_B), yc.reshape(b * s, W_C), lw, tm=tm_p, alpha=alpha)
        for lst, a in zip(st_p, (ckv, kr, kk, vv)):
            lst.append(a)
        ya, ql, qr, kcat, sbq, sbk, sbv, ckv, kr, kk, vv, gv = _in_proj(
            xs, lw_dec, cos_s, sin_s, tm=t_dec, want_gv=True)
        ql_s = ql.reshape(db, ds, H_B, KV_LORA).transpose(0, 2, 1, 3).reshape(db, H_B * ds, KV_LORA)
        qr_s = qr.reshape(db, ds, H_B, ROPE_DIM).transpose(0, 2, 1, 3).reshape(db, H_B * ds, ROPE_DIM)
        pad_new = lambda a: jnp.pad(a.reshape(db, ds, a.shape[-1]), ((0, 0), (0, LANES - ds), (0, 0)))
        yb = _mla_decode(ql_s, qr_s, cache_mla_ckv, cache_mla_krope, l, pad_new(ckv), pad_new(kr),
                         lw['w_uv'], lw['g_b'], tk=tk_dec)
        yc = _sb_decode(sbq, cache_k2, cache_v2, l, pad_new(sbk), pad_new(sbv), tri_dp, tri_dn,
                        lw['g_c'], n_new=ds, tk=tk_dec)
        xs = _out_mlp(xs, ya, yb.reshape(t_dec, W_B), yc, lw, tm=t_dec, alpha=alpha)
        for lst, a in zip(st_s, (ckv, kr, kk, vv, gv)):
            lst.append(a)

    def stack_p(lst, tail):
        return jnp.stack(lst).reshape((depth, b, s) + tail)

    def stack_s(lst, tail):
        return jnp.stack(lst).reshape((depth, db, ds) + tail)

    return (xp.reshape(b, s, d), xs.reshape(db, ds, d),
            stack_p(st_p[0], (KV_LORA,)), stack_p(st_p[1], (ROPE_DIM,)),
            stack_p(st_p[2], (H_C, D_C)), stack_p(st_p[3], (H_C, D_C)),
            stack_s(st_s[0], (KV_LORA,)), stack_s(st_s[1], (ROPE_DIM,)),
            stack_s(st_s[2], (H_C, D_C)), stack_s(st_s[3], (H_C, D_C)),
            stack_s(st_s[4], (G_A, DG_A)))
```

```python
import functools

import jax
import jax.numpy as jnp
from jax import lax
from jax.experimental import pallas as pl
from jax.experimental.pallas import tpu as pltpu

F32 = jnp.float32
BF16 = jnp.bfloat16

CHUNK = 64
GMLP_CHUNK = 128
G_A, DG_A = 4, 64
W_A = G_A * DG_A
H_B, NOPE_DIM, ROPE_DIM, V_DIM = 4, 128, 64, 128
W_B = H_B * V_DIM
Q_LORA, KV_LORA = 384, 256
ROPE_THETA = 10000.0
LOG2E = 1.4426950408889634
MLA_SCALE = (NOPE_DIM + ROPE_DIM) ** -0.5
MLA_QSCALE = MLA_SCALE * LOG2E
H_C, D_C = 4, 64
W_C = H_C * D_C
SB_SCALE = D_C ** -0.5
SB_QSCALE = SB_SCALE * LOG2E
EPS = 1e-5
NEG_BIG = -1e30

LANES = 128
SB_ROW_GROUPS = 1
VMEM_LIMIT_BYTES = 56 * 1024 * 1024


def _dot(a, b):
    return jnp.dot(a, b, preferred_element_type=F32)


def _dot_nt(a, b):
    return lax.dot_general(a, b, (((1,), (1,)), ((), ())), preferred_element_type=F32)


def _rep(x, n):
    return x if n == 1 else jnp.concatenate([x] * n, axis=1)


def _rms(x, g):
    ms = jnp.mean(x * x, axis=-1, keepdims=True)
    return x * lax.rsqrt(ms + EPS) * g


def _layer_norm(x, g, b):
    mu = jnp.mean(x, axis=-1, keepdims=True)
    d = x - mu
    var = jnp.mean(d * d, axis=-1, keepdims=True)
    return d * lax.rsqrt(var + EPS) * g + b


def _gelu_tanh(x):
    c = 0.7978845608028654
    return 0.5 * x * (1.0 + jnp.tanh(c * (x + 0.044715 * (x * x * x))))


def _head_lane_id(width, per_head):
    return lax.broadcasted_iota(jnp.int32, (1, width), 1) // per_head


def _in_proj_kernel(x_ref, wa_ref, wcq_ref, wckv_ref, wkr_ref, wc_ref, gcq_ref, gckv_ref,
                    wuqn_ref, wuqr_ref, wukt_ref, mixw_ref, mixb_ref, ga_ref, seg_ref,
                    cos_ref, sin_ref,
                    ya_ref, ql_ref, qr_ref, kcat_ref, sbq_ref, sbk_ref, sbv_ref,
                    ckv_ref, kr_ref, k_ref, v_ref, *gv_refs, tm):
    xb = x_ref[...].astype(BF16)
    grp = _head_lane_id(W_A, DG_A)
    lower_half = (lax.broadcasted_iota(jnp.int32, (1, 256), 1) % ROPE_DIM) < (ROPE_DIM // 2)
    cos = cos_ref[...]
    sin = sin_ref[...]

    def seg_mean(val):
        hi = val.astype(BF16)
        lo = (val - hi.astype(F32)).astype(BF16)
        return (_dot(hi, seg_ref[...]) + _dot(lo, seg_ref[...])) * (1.0 / DG_A)

    def rope(val):
        swapped = jnp.where(lower_half, pltpu.roll(val, 256 - ROPE_DIM // 2, 1),
                            pltpu.roll(val, ROPE_DIM // 2, 1))
        return val * cos + swapped * sin

    pa = _dot(xb, wa_ref[...])
    u = _gelu_tanh(pa[:, :W_A])
    gv = _gelu_tanh(pa[:, W_A:])
    mu = seg_mean(gv)
    dv = gv - mu
    v = dv * lax.rsqrt(seg_mean(dv * dv) + EPS)
    if gv_refs:
        gv_refs[0][...] = v
    vb = v.astype(BF16)
    zero_b = jnp.zeros((GMLP_CHUNK, W_A), BF16)
    mixed = []
    for c in range(tm // GMLP_CHUNK):
        vc = vb[c * GMLP_CHUNK:(c + 1) * GMLP_CHUNK, :]
        stacked = jnp.concatenate([jnp.where(grp == g, vc, zero_b) for g in range(G_A)], axis=0)
        mixed.append(_dot(mixw_ref[...], stacked) + mixb_ref[...])
    y_a = u * jnp.concatenate(mixed, axis=0)
    ya_ref[...] = _rms(y_a, ga_ref[...]).astype(BF16)

    cqn = _rms(_dot(xb, wcq_ref[...]), gcq_ref[...]).astype(BF16)
    qn = _dot(cqn, wuqn_ref[...])
    qr = rope(_dot(cqn, wuqr_ref[...]))
    qr_ref[...] = (qr * MLA_QSCALE).astype(BF16)
    for h in range(H_B):
        ql = _dot(qn[:, h * NOPE_DIM:(h + 1) * NOPE_DIM].astype(BF16), wukt_ref[h])
        ql_ref[:, h * KV_LORA:(h + 1) * KV_LORA] = (ql * MLA_QSCALE).astype(BF16)
    ckv = _rms(_dot(xb, wckv_ref[...]), gckv_ref[...])
    ckv_ref[...] = ckv
    kr4 = rope(_dot(xb, wkr_ref[...]))
    kr_ref[...] = kr4[:, :ROPE_DIM]
    kcat_ref[:, :KV_LORA] = ckv.astype(BF16)
    kcat_ref[:, KV_LORA:] = kr4.astype(BF16)

    pc = _dot(xb, wc_ref[...])
    sbq_ref[...] = (pc[:, :W_C] * SB_QSCALE).astype(BF16)
    kc = pc[:, W_C:2 * W_C]
    vc_ = pc[:, 2 * W_C:]
    k_ref[...] = kc
    v_ref[...] = vc_
    sbk_ref[...] = kc.astype(BF16)
    sbv_ref[...] = vc_.astype(BF16)


def _in_proj(x2d, lw, cos_tab, sin_tab, *, tm, want_gv):
    t = x2d.shape[0]
    n_pos_blocks = cos_tab.shape[0] // tm
    tok = lambda w: pl.BlockSpec((tm, w), lambda i: (i, 0))
    full = lambda a: pl.BlockSpec(a.shape, lambda i: (0,) * a.ndim)
    pos = pl.BlockSpec((tm, 256), lambda i: (i % n_pos_blocks, 0))
    weights = (lw['w_a'], lw['w_cq'], lw['w_ckv'], lw['w_kr4'], lw['w_c'], lw['g_cq'], lw['g_ckv'],
               lw['w_uqn'], lw['w_uqr'], lw['w_ukt'], lw['mixw'], lw['mixb'], lw['g_a'], lw['seg'])
    out_shape = [
        jax.ShapeDtypeStruct((t, W_A), BF16),
        jax.ShapeDtypeStruct((t, H_B * KV_LORA), BF16),
        jax.ShapeDtypeStruct((t, H_B * ROPE_DIM), BF16),
        jax.ShapeDtypeStruct((t, 2 * KV_LORA), BF16),
        jax.ShapeDtypeStruct((t, W_C), BF16),
        jax.ShapeDtypeStruct((t, W_C), BF16),
        jax.ShapeDtypeStruct((t, W_C), BF16),
        jax.ShapeDtypeStruct((t, KV_LORA), F32),
        jax.ShapeDtypeStruct((t, ROPE_DIM), F32),
        jax.ShapeDtypeStruct((t, W_C), F32),
        jax.ShapeDtypeStruct((t, W_C), F32),
    ]
    if want_gv:
        out_shape.append(jax.ShapeDtypeStruct((t, W_A), F32))
    return pl.pallas_call(
        functools.partial(_in_proj_kernel, tm=tm),
        grid=(t // tm,),
        in_specs=[tok(x2d.shape[1])] + [full(w) for w in weights] + [pos, pos],
        out_specs=[tok(s.shape[1]) for s in out_shape],
        out_shape=out_shape,
        compiler_params=pltpu.CompilerParams(dimension_semantics=("parallel",),
                                             vmem_limit_bytes=VMEM_LIMIT_BYTES),
        name="in_proj",
    )(x2d, *weights, cos_tab, sin_tab)


def _mla_finish(acc, l_inv, wuv_ref, gb_ref, rows):
    out = acc * _rep(l_inv, KV_LORA // LANES)
    ys = [_dot(out[h * rows:(h + 1) * rows].astype(BF16), wuv_ref[h]) for h in range(H_B)]
    return _rms(jnp.concatenate(ys, axis=1), gb_ref[...]).astype(BF16)


def _mla_kernel(ql_ref, qr_ref, kcat_ref, hide_ref, wuv_ref, gb_ref, o_ref, qs_ref, sa_ref, sb_ref,
                m_ref, l_ref, acc_ref, *, tq):
    qi = pl.program_id(1)
    head = _head_lane_id(H_B * ROPE_DIM, ROPE_DIM)
    qr = qr_ref[0]
    for h in range(H_B):
        qs_ref[h * tq:(h + 1) * tq, :KV_LORA] = ql_ref[0, :, h * KV_LORA:(h + 1) * KV_LORA]
        qs_ref[h * tq:(h + 1) * tq, KV_LORA:] = jnp.where(head == h, qr, jnp.zeros_like(qr))

    m_ref[...] = jnp.full(m_ref.shape, NEG_BIG, F32)
    l_ref[...] = jnp.zeros(l_ref.shape, F32)
    acc_ref[...] = jnp.zeros(acc_ref.shape, F32)

    def key_block(j):
        return kcat_ref[0, pl.ds(pl.multiple_of(j * tq, tq), tq), :]

    def scores_into(dst_ref, j):
        dst_ref[...] = _dot_nt(qs_ref[...], key_block(j))

    def consume(src_ref, j, diagonal):
        s = src_ref[...]
        if diagonal:
            s = s + hide_ref[...]
        m_prev = m_ref[...]
        m_next = jnp.maximum(m_prev, jnp.max(s, axis=1, keepdims=True))
        p = jnp.exp2(s - _rep(m_next, tq // LANES))
        alpha = jnp.exp2(m_prev - m_next)
        l_ref[...] = alpha * l_ref[...] + jnp.sum(p, axis=1, keepdims=True)
        m_ref[...] = m_next
        acc_ref[...] = (acc_ref[...] * _rep(alpha, KV_LORA // LANES)
                        + _dot(p.astype(BF16), key_block(j)[:, :KV_LORA]))

    scores_into(sa_ref, 0)

    def pair(t, carry):
        scores_into(sb_ref, 2 * t + 1)
        consume(sa_ref, 2 * t, False)
        scores_into(sa_ref, 2 * t + 2)
        consume(sb_ref, 2 * t + 1, False)
        return carry

    lax.fori_loop(0, qi // 2, pair, 0)

    @pl.when(qi % 2 == 0)
    def _():
        consume(sa_ref, qi, True)

    @pl.when(qi % 2 == 1)
    def _():
        scores_into(sb_ref, qi)
        consume(sa_ref, qi - 1, False)
        consume(sb_ref, qi, True)

    o_ref[0] = _mla_finish(acc_ref[...], 1.0 / l_ref[...], wuv_ref, gb_ref, tq)


def _mla_prompt(ql, qr, kcat, w_uv, g_b, *, tq):
    b, s, _ = ql.shape
    rows = H_B * tq
    row = lax.broadcasted_iota(jnp.int32, (rows, tq), 0) % tq
    col = lax.broadcasted_iota(jnp.int32, (rows, tq), 1)
    hide = jnp.where((col // CHUNK) <= (row // CHUNK), 0.0, NEG_BIG).astype(F32)
    return pl.pallas_call(
        functools.partial(_mla_kernel, tq=tq),
        grid=(b, s // tq),
        in_specs=[
            pl.BlockSpec((1, tq, H_B * KV_LORA), lambda bi, qi: (bi, qi, 0)),
            pl.BlockSpec((1, tq, H_B * ROPE_DIM), lambda bi, qi: (bi, qi, 0)),
            pl.BlockSpec((1, s, 2 * KV_LORA), lambda bi, qi: (bi, 0, 0)),
            pl.BlockSpec(hide.shape, lambda bi, qi: (0, 0)),
            pl.BlockSpec(w_uv.shape, lambda bi, qi: (0, 0, 0)),
            pl.BlockSpec(g_b.shape, lambda bi, qi: (0, 0)),
        ],
        out_specs=pl.BlockSpec((1, tq, W_B), lambda bi, qi: (bi, qi, 0)),
        out_shape=jax.ShapeDtypeStruct((b, s, W_B), BF16),
        scratch_shapes=[pltpu.VMEM((rows, 2 * KV_LORA), BF16),
                        pltpu.VMEM((rows, tq), F32),
                        pltpu.VMEM((rows, tq), F32),
                        pltpu.VMEM((rows, LANES), F32),
                        pltpu.VMEM((rows, LANES), F32),
                        pltpu.VMEM((rows, KV_LORA), F32)],
        compiler_params=pltpu.CompilerParams(dimension_semantics=("parallel", "arbitrary"),
                                             vmem_limit_bytes=VMEM_LIMIT_BYTES),
        name="mla_prompt",
    )(ql, qr, kcat, hide, w_uv, g_b)


def _sb_block(qs, kb, vb, tri, carry, vis01=None, hide=None):
    z = _dot_nt(qs, kb)
    neg_abs = lax.bitcast_convert_type(
        lax.bitcast_convert_type(z, jnp.uint32) | jnp.uint32(0x80000000), F32)
    skip = jnp.maximum(z, 0.0) + jnp.log(1.0 + jnp.exp2(neg_abs)) * LOG2E
    log_take = z - skip
    if vis01 is not None:
        skip = skip * vis01
    later = _dot(skip.astype(BF16), tri)
    arg = log_take - later - _rep(carry, z.shape[1] // LANES)
    if hide is not None:
        arg = arg + hide
    w = jnp.exp2(arg)
    return _dot(w.astype(BF16), vb), carry + jnp.sum(skip, axis=1, keepdims=True)


def _strict_causal_masks(rows_per_head, n_rows, n_cols):
    row = lax.broadcasted_iota(jnp.int32, (n_rows, n_cols), 0) % rows_per_head
    col = lax.broadcasted_iota(jnp.int32, (n_rows, n_cols), 1)
    vis = col < row
    return jnp.where(vis, 1.0, 0.0).astype(F32), jnp.where(vis, 0.0, NEG_BIG).astype(F32)


def _stack_heads_masked(q, per_head):
    head = _head_lane_id(q.shape[1], per_head)
    return jnp.concatenate([jnp.where(head == h, q, jnp.zeros_like(q)) for h in range(H_C)], axis=0)


def _unstack_heads(acc, rows, per_head):
    head = _head_lane_id(acc.shape[1], per_head)
    y = jnp.zeros((rows, acc.shape[1]), F32)
    for h in range(H_C):
        y = y + jnp.where(head == h, acc[h * rows:(h + 1) * rows], 0.0)
    return y


def _sb_kernel(q_ref, k_ref, v_ref, tri_ref, vis_ref, hide_ref, gc_ref, o_ref, qs_ref, c_ref,
               acc_ref, *, tq):
    qi = pl.program_id(1)
    qs_ref[...] = _stack_heads_masked(q_ref[0], D_C)

    def step(j, masked, first):
        off = pl.multiple_of(j * tq, tq)
        kb = k_ref[0, pl.ds(off, tq), :]
        vb = v_ref[0, pl.ds(off, tq), :]
        for part in range(SB_ROW_GROUPS):
            r = (H_C * tq) // SB_ROW_GROUPS
            rows = slice(part * r, (part + 1) * r)
            masks = (vis_ref[rows, :], hide_ref[rows, :]) if masked else ()
            carry_in = jnp.zeros((r, LANES), F32) if first else c_ref[rows, :]
            pv, carry = _sb_block(qs_ref[rows, :], kb, vb, tri_ref[...], carry_in, *masks)
            acc_ref[rows, :] = pv if first else acc_ref[rows, :] + pv
            c_ref[rows, :] = carry

    step(qi, True, True)

    def body(jj, carry):
        step(qi - 1 - jj, False, False)
        return carry

    lax.fori_loop(0, qi, body, 0)
    o_ref[0] = _rms(_unstack_heads(acc_ref[...], tq, D_C), gc_ref[...]).astype(BF16)


def _sb_prompt(q, k, v, tri, g_c, *, tq):
    b, s, _ = q.shape
    rows = H_C * tq
    vis01, hide = _strict_causal_masks(tq, rows, tq)
    return pl.pallas_call(
        functools.partial(_sb_kernel, tq=tq),
        grid=(b, s // tq),
        in_specs=[
            pl.BlockSpec((1, tq, W_C), lambda bi, qi: (bi, qi, 0)),
            pl.BlockSpec((1, s, W_C), lambda bi, qi: (bi, 0, 0)),
            pl.BlockSpec((1, s, W_C), lambda bi, qi: (bi, 0, 0)),
            pl.BlockSpec(tri.shape, lambda bi, qi: (0, 0)),
            pl.BlockSpec(vis01.shape, lambda bi, qi: (0, 0)),
            pl.BlockSpec(hide.shape, lambda bi, qi: (0, 0)),
            pl.BlockSpec(g_c.shape, lambda bi, qi: (0, 0)),
        ],
        out_specs=pl.BlockSpec((1, tq, W_C), lambda bi, qi: (bi, qi, 0)),
        out_shape=jax.ShapeDtypeStruct((b, s, W_C), BF16),
        scratch_shapes=[pltpu.VMEM((rows, W_C), BF16),
                        pltpu.VMEM((rows, LANES), F32),
                        pltpu.VMEM((rows, W_C), F32)],
        compiler_params=pltpu.CompilerParams(dimension_semantics=("parallel", "arbitrary"),
                                             vmem_limit_bytes=VMEM_LIMIT_BYTES),
        name="sb_prompt",
    )(q, k, v, tri, vis01, hide, g_c)


def _mla_dec_kernel(ql_ref, qr_ref, ckv_p_ref, kr_p_ref, ckv_n_ref, kr_n_ref, wuv_ref, gb_ref,
                    o_ref, m_ref, l_ref, acc_ref, *, n_new):
    kb_i = pl.program_id(1)

    @pl.when(kb_i == 0)
    def _():
        m_ref[...] = jnp.full(m_ref.shape, NEG_BIG, F32)
        l_ref[...] = jnp.zeros(l_ref.shape, F32)
        acc_ref[...] = jnp.zeros(acc_ref.shape, F32)

    def update(ckv_f32, kr_f32, n_valid):
        ckv_b = ckv_f32.astype(BF16)
        s = _dot_nt(ql_ref[0], ckv_b) + _dot_nt(qr_ref[0], kr_f32.astype(BF16))
        if n_valid is not None:
            s = jnp.where(lax.broadcasted_iota(jnp.int32, s.shape, 1) < n_valid, s, NEG_BIG)
        m_prev = m_ref[...]
        m_next = jnp.maximum(m_prev, jnp.max(s, axis=1, keepdims=True))
        p = jnp.exp2(s - _rep(m_next, s.shape[1] // LANES))
        alpha = jnp.exp2(m_prev - m_next)
        l_ref[...] = alpha * l_ref[...] + jnp.sum(p, axis=1, keepdims=True)
        m_ref[...] = m_next
        acc_ref[...] = acc_ref[...] * _rep(alpha, KV_LORA // LANES) + _dot(p.astype(BF16), ckv_b)

    update(ckv_p_ref[...], kr_p_ref[...], None)

    @pl.when(kb_i == pl.num_programs(1) - 1)
    def _():
        update(ckv_n_ref[0], kr_n_ref[0], n_new)
        o_ref[0] = _mla_finish(acc_ref[...], 1.0 / l_ref[...], wuv_ref, gb_ref, n_new)


def _mla_decode(ql_s, qr_s, cache_ckv, cache_kr, layer, ckv_new, kr_new, w_uv, g_b, *, tk):
    b, rows, _ = ql_s.shape
    n_new = rows // H_B
    past = cache_ckv.shape[2]
    return pl.pallas_call(
        functools.partial(_mla_dec_kernel, n_new=n_new),
        grid=(b, past // tk),
        in_specs=[
            pl.BlockSpec((1, rows, KV_LORA), lambda bi, ki: (bi, 0, 0)),
            pl.BlockSpec((1, rows, ROPE_DIM), lambda bi, ki: (bi, 0, 0)),
            pl.BlockSpec((None, None, tk, KV_LORA), lambda bi, ki: (layer, bi, ki, 0)),
            pl.BlockSpec((None, None, tk, ROPE_DIM), lambda bi, ki: (layer, bi, ki, 0)),
            pl.BlockSpec((1,) + ckv_new.shape[1:], lambda bi, ki: (bi, 0, 0)),
            pl.BlockSpec((1,) + kr_new.shape[1:], lambda bi, ki: (bi, 0, 0)),
            pl.BlockSpec(w_uv.shape, lambda bi, ki: (0, 0, 0)),
            pl.BlockSpec(g_b.shape, lambda bi, ki: (0, 0)),
        ],
        out_specs=pl.BlockSpec((1, n_new, W_B), lambda bi, ki: (bi, 0, 0)),
        out_shape=jax.ShapeDtypeStruct((b, n_new, W_B), BF16),
        scratch_shapes=[pltpu.VMEM((rows, LANES), F32),
                        pltpu.VMEM((rows, LANES), F32),
                        pltpu.VMEM((rows, KV_LORA), F32)],
        compiler_params=pltpu.CompilerParams(dimension_semantics=("parallel", "arbitrary"),
                                             vmem_limit_bytes=VMEM_LIMIT_BYTES),
        name="mla_decode",
    )(ql_s, qr_s, cache_ckv, cache_kr, ckv_new, kr_new, w_uv, g_b)


def _sb_dec_kernel(q_ref, k_p_ref, v_p_ref, k_n_ref, v_n_ref, tri_p_ref, tri_n_ref, gc_ref,
                   o_ref, qs_ref, c_ref, acc_ref, *, n_new):
    kb_i = pl.program_id(1)

    @pl.when(kb_i == 0)
    def _():
        qs_ref[...] = _stack_heads_masked(q_ref[...], D_C)
        vis01, hide = _strict_causal_masks(n_new, H_C * n_new, k_n_ref.shape[1])
        pv, carry = _sb_block(qs_ref[...], k_n_ref[0], v_n_ref[0], tri_n_ref[...],
                              jnp.zeros(c_ref.shape, F32), vis01, hide)
        acc_ref[...] = pv
        c_ref[...] = carry

    pv, carry = _sb_block(qs_ref[...], k_p_ref[...].astype(BF16), v_p_ref[...].astype(BF16),
                          tri_p_ref[...], c_ref[...], None)
    acc_ref[...] += pv
    c_ref[...] = carry

    @pl.when(kb_i == pl.num_programs(1) - 1)
    def _():
        o_ref[...] = _rms(_unstack_heads(acc_ref[...], n_new, D_C), gc_ref[...]).astype(BF16)


def _sb_decode(q, cache_k, cache_v, layer, k_new, v_new, tri_p, tri_n, g_c, *, n_new, tk):
    t = q.shape[0]
    b = t // n_new
    nkb = cache_k.shape[2] // tk
    rows = H_C * n_new
    return pl.pallas_call(
        functools.partial(_sb_dec_kernel, n_new=n_new),
        grid=(b, nkb),
        in_specs=[
            pl.BlockSpec((n_new, W_C), lambda bi, ki: (bi, 0)),
            pl.BlockSpec((None, None, tk, W_C), lambda bi, ki: (layer, bi, nkb - 1 - ki, 0)),
            pl.BlockSpec((None, None, tk, W_C), lambda bi, ki: (layer, bi, nkb - 1 - ki, 0)),
            pl.BlockSpec((1,) + k_new.shape[1:], lambda bi, ki: (bi, 0, 0)),
            pl.BlockSpec((1,) + v_new.shape[1:], lambda bi, ki: (bi, 0, 0)),
            pl.BlockSpec(tri_p.shape, lambda bi, ki: (0, 0)),
            pl.BlockSpec(tri_n.shape, lambda bi, ki: (0, 0)),
            pl.BlockSpec(g_c.shape, lambda bi, ki: (0, 0)),
        ],
        out_specs=pl.BlockSpec((n_new, W_C), lambda bi, ki: (bi, 0)),
        out_shape=jax.ShapeDtypeStruct((t, W_C), BF16),
        scratch_shapes=[pltpu.VMEM((rows, W_C), BF16),
                        pltpu.VMEM((rows, LANES), F32),
                        pltpu.VMEM((rows, W_C), F32)],
        compiler_params=pltpu.CompilerParams(dimension_semantics=("parallel", "arbitrary"),
                                             vmem_limit_bytes=VMEM_LIMIT_BYTES),
        name="sb_decode",
    )(q, cache_k, cache_v, k_new, v_new, tri_p, tri_n, g_c)


def _out_mlp_kernel(x_ref, ya_ref, yb_ref, yc_ref, wout_ref, g1_ref, b1_ref, wup_ref, bup_ref,
                    wdown_ref, bdown_ref, g2_ref, b2_ref, o_ref, *, alpha, ff_chunk):
    y = jnp.concatenate([ya_ref[...], yb_ref[...], yc_ref[...]], axis=1)
    x1 = _layer_norm(alpha * x_ref[...] + _dot(y, wout_ref[...]), g1_ref[...], b1_ref[...])
    x1b = x1.astype(BF16)
    h = jnp.zeros(x1.shape, F32)
    for c in range(wup_ref.shape[1] // ff_chunk):
        sl = slice(c * ff_chunk, (c + 1) * ff_chunk)
        a = jnp.maximum(_dot(x1b, wup_ref[:, sl]) + bup_ref[:, sl], 0.0)
        h = h + _dot((a * a).astype(BF16), wdown_ref[sl, :])
    o_ref[...] = _layer_norm(alpha * x1 + h + bdown_ref[...], g2_ref[...], b2_ref[...])


def _out_mlp(x2d, ya, yb, yc, lw, *, tm, alpha):
    t, d = x2d.shape
    tok = lambda w: pl.BlockSpec((tm, w), lambda i: (i, 0))
    const = lambda a: pl.BlockSpec(a.shape, lambda i: (0,) * a.ndim, pipeline_mode=pl.Buffered(1))
    weights = (lw['w_out'], lw['ln1_g'], lw['ln1_b'], lw['w_up'], lw['b_up'], lw['w_down'],
               lw['b_down'], lw['ln2_g'], lw['ln2_b'])
    return pl.pallas_call(
        functools.partial(_out_mlp_kernel, alpha=alpha, ff_chunk=1024),
        grid=(t // tm,),
        in_specs=[tok(d), tok(W_A), tok(W_B), tok(W_C)] + [const(w) for w in weights],
        out_specs=tok(d),
        out_shape=jax.ShapeDtypeStruct((t, d), F32),
        compiler_params=pltpu.CompilerParams(dimension_semantics=("parallel",),
                                             vmem_limit_bytes=VMEM_LIMIT_BYTES),
        name="out_mlp",
    )(x2d, ya, yb, yc, *weights)


def _rope_tables(pos):
    half = ROPE_DIM // 2
    inv = ROPE_THETA ** (-jnp.arange(half, dtype=F32) / half)
    ang = pos.astype(F32)[:, None] * inv[None, :]
    cos, sin = jnp.cos(ang), jnp.sin(ang)
    cos_t = jnp.tile(jnp.concatenate([cos, cos], -1), (1, H_B))
    sin_t = jnp.tile(jnp.concatenate([-sin, sin], -1), (1, H_B))
    return cos_t, sin_t


def _strict_lower_ones(n):
    j = lax.broadcasted_iota(jnp.int32, (n, n), 0)
    s = lax.broadcasted_iota(jnp.int32, (n, n), 1)
    return (j > s).astype(BF16)


def _row(a):
    return a.reshape(1, -1).astype(F32)


def _prep_layer(l, w_in, w_s, b_s, g_cq, g_ckv, w_uq, w_uk, w_uv, g_mix, w_out, ln1_g, ln1_b,
                w_up, b_up, w_down, b_down, ln2_g, ln2_b, n_dec):
    wi = w_in[l].astype(BF16)
    o_cq, o_ckv, o_kr, o_c = 2 * W_A, 2 * W_A + Q_LORA, 2 * W_A + Q_LORA + KV_LORA, \
        2 * W_A + Q_LORA + KV_LORA + ROPE_DIM
    uq = w_uq[l].astype(BF16)
    lw = {
        'w_a': wi[:, :o_cq], 'w_cq': wi[:, o_cq:o_ckv], 'w_ckv': wi[:, o_ckv:o_kr],
        'w_kr4': jnp.tile(wi[:, o_kr:o_c], (1, H_B)), 'w_c': wi[:, o_c:],
        'g_cq': _row(g_cq[l]), 'g_ckv': _row(g_ckv[l]),
        'w_uqn': uq[:, :, :NOPE_DIM].reshape(Q_LORA, H_B * NOPE_DIM),
        'w_uqr': uq[:, :, NOPE_DIM:].reshape(Q_LORA, H_B * ROPE_DIM),
        'w_ukt': jnp.swapaxes(w_uk[l], 1, 2).astype(BF16),
        'w_uv': w_uv[l].astype(BF16),
        'g_a': _row(g_mix[l, :W_A]), 'g_b': _row(g_mix[l, W_A:W_A + W_B]),
        'g_c': _row(g_mix[l, W_A + W_B:]),
        'w_out': w_out[l].astype(BF16), 'ln1_g': _row(ln1_g[l]), 'ln1_b': _row(ln1_b[l]),
        'w_up': w_up[l].astype(BF16), 'b_up': _row(b_up[l]), 'w_down': w_down[l].astype(BF16),
        'b_down': _row(b_down[l]), 'ln2_g': _row(ln2_g[l]), 'ln2_b': _row(ln2_b[l]),
    }
    seg = lax.broadcasted_iota(jnp.int32, (W_A, W_A), 0) // DG_A
    lw['seg'] = (seg == seg.T).astype(BF16)
    idx = jnp.arange(GMLP_CHUNK)
    vis = (idx[None, :] // CHUNK) <= (idx[:, None] // CHUNK)
    wm = jnp.where(vis[None], w_s[l], 0.0)
    lw['mixw'] = jnp.transpose(wm, (1, 0, 2)).reshape(GMLP_CHUNK, G_A * GMLP_CHUNK).astype(BF16)
    lw['mixb'] = jnp.repeat(b_s[l].T, DG_A, axis=1).astype(F32)
    reps = GMLP_CHUNK // n_dec
    wd = w_s[l][:, :n_dec, :n_dec]
    wmd = jax.vmap(lambda w: jnp.kron(jnp.eye(reps, dtype=F32), w))(wd)
    lw_dec = dict(lw)
    lw_dec['mixw'] = jnp.transpose(wmd, (1, 0, 2)).reshape(GMLP_CHUNK, G_A * GMLP_CHUNK).astype(BF16)
    lw_dec['mixb'] = jnp.repeat(jnp.tile(b_s[l][:, :n_dec].T, (reps, 1)), DG_A, axis=1).astype(F32)
    return lw, lw_dec


def kernel(x_prompt, x_sample, cache_mla_ckv, cache_mla_krope, cache_sb_k, cache_sb_v, w_in, w_s, b_s, g_cq, g_ckv, w_uq, w_uk, w_uv, g_mix, w_out, ln1_g, ln1_b, w_up, b_up, w_down, b_down, ln2_g, ln2_b):
    b, s, d = x_prompt.shape
    db, ds, _ = x_sample.shape
    depth = w_in.shape[0]
    past = cache_mla_ckv.shape[2]
    alpha = (2 * depth) ** 0.25
    tm_p, tq = 512, 256
    t_dec = db * ds
    assert t_dec == GMLP_CHUNK and s % tm_p == 0 and past % CHUNK == 0 and ds <= CHUNK

    cos_p, sin_p = _rope_tables(jnp.arange(s, dtype=jnp.int32))
    cos_s, sin_s = _rope_tables(jnp.tile(past + jnp.arange(ds, dtype=jnp.int32), db))
    tri_p = _strict_lower_ones(tq)
    tk_dec = 1024
    tri_dp = _strict_lower_ones(tk_dec)
    tri_dn = _strict_lower_ones(LANES)
    cache_k2 = cache_sb_k.reshape(depth, db, past, W_C)
    cache_v2 = cache_sb_v.reshape(depth, db, past, W_C)

    xp = x_prompt.reshape(b * s, d)
    xs = x_sample.reshape(t_dec, d)
    st_p = [[] for _ in range(4)]
    st_s = [[] for _ in range(5)]
    for l in range(depth):
        lw, lw_dec = _prep_layer(l, w_in, w_s, b_s, g_cq, g_ckv, w_uq, w_uk, w_uv, g_mix, w_out,
                                 ln1_g, ln1_b, w_up, b_up, w_down, b_down, ln2_g, ln2_b, ds)
        ya, ql, qr, kcat, sbq, sbk, sbv, ckv, kr, kk, vv = _in_proj(
            xp, lw, cos_p, sin_p, tm=tm_p, want_gv=False)
        r3 = lambda a: a.reshape(b, s, a.shape[-1])
        yb = _mla_prompt(r3(ql), r3(qr), r3(kcat), lw['w_uv'], lw['g_b'], tq=tq)
        yc = _sb_prompt(r3(sbq), r3(sbk), r3(sbv), tri_p, lw['g_c'], tq=tq)
        xp = _out_mlp(xp, ya, yb.reshape(b * s, W_B), yc.reshape(b * s, W_C), lw, tm=tm_p, alpha=alpha)
        for lst, a in zip(st_p, (ckv, kr, kk, vv)):
            lst.append(a)
        ya, ql, qr, kcat, sbq, sbk, sbv, ckv, kr, kk, vv, gv = _in_proj(
            xs, lw_dec, cos_s, sin_s, tm=t_dec, want_gv=True)
        ql_s = ql.reshape(db, ds, H_B, KV_LORA).transpose(0, 2, 1, 3).reshape(db, H_B * ds, KV_LORA)
        qr_s = qr.reshape(db, ds, H_B, ROPE_DIM).transpose(0, 2, 1, 3).reshape(db, H_B * ds, ROPE_DIM)
        pad_new = lambda a: jnp.pad(a.reshape(db, ds, a.shape[-1]), ((0, 0), (0, LANES - ds), (0, 0)))
        yb = _mla_decode(ql_s, qr_s, cache_mla_ckv, cache_mla_krope, l, pad_new(ckv), pad_new(kr),
                         lw['w_uv'], lw['g_b'], tk=tk_dec)
        yc = _sb_decode(sbq, cache_k2, cache_v2, l, pad_new(sbk), pad_new(sbv), tri_dp, tri_dn,
                        lw['g_c'], n_new=ds, tk=tk_dec)
        xs = _out_mlp(xs, ya, yb.reshape(t_dec, W_B), yc, lw, tm=t_dec, alpha=alpha)
        for lst, a in zip(st_s, (ckv, kr, kk, vv, gv)):
            lst.append(a)

    def stack_p(lst, tail):
        return jnp.stack(lst).reshape((depth, b, s) + tail)

    def stack_s(lst, tail):
        return jnp.stack(lst).reshape((depth, db, ds) + tail)

    return (xp.reshape(b, s, d), xs.reshape(db, ds, d),
            stack_p(st_p[0], (KV_LORA,)), stack_p(st_p[1], (ROPE_DIM,)),
            stack_p(st_p[2], (H_C, D_C)), stack_p(st_p[3], (H_C, D_C)),
            stack_s(st_s[0], (KV_LORA,)), stack_s(st_s[1], (ROPE_DIM,)),
            stack_s(st_s[2], (H_C, D_C)), stack_s(st_s[3], (H_C, D_C)),
            stack_s(st_s[4], (G_A, DG_A)))
```

```python
import functools

import jax
import jax.numpy as jnp
from jax import lax
from jax.experimental import pallas as pl
from jax.experimental.pallas import tpu as pltpu

F32 = jnp.float32
BF16 = jnp.bfloat16

CHUNK = 64
GMLP_CHUNK = 128
G_A, DG_A = 4, 64
W_A = G_A * DG_A
H_B, NOPE_DIM, ROPE_DIM, V_DIM = 4, 128, 64, 128
W_B = H_B * V_DIM
Q_LORA, KV_LORA = 384, 256
ROPE_THETA = 10000.0
LOG2E = 1.4426950408889634
MLA_SCALE = (NOPE_DIM + ROPE_DIM) ** -0.5
MLA_QSCALE = MLA_SCALE * LOG2E
H_C, D_C = 4, 64
W_C = H_C * D_C
SB_SCALE = D_C ** -0.5
SB_QSCALE = SB_SCALE * LOG2E
EPS = 1e-5
NEG_BIG = -1e30

LANES = 128
VMEM_LIMIT_BYTES = 56 * 1024 * 1024


def _dot(a, b):
    return jnp.dot(a, b, preferred_element_type=F32)


def _dot_nt(a, b):
    return lax.dot_general(a, b, (((1,), (1,)), ((), ())), preferred_element_type=F32)


def _rep(x, n):
    return x if n == 1 else jnp.concatenate([x] * n, axis=1)


def _rms(x, g):
    ms = jnp.mean(x * x, axis=-1, keepdims=True)
    return x * lax.rsqrt(ms + EPS) * g


def _layer_norm(x, g, b):
    mu = jnp.mean(x, axis=-1, keepdims=True)
    d = x - mu
    var = jnp.mean(d * d, axis=-1, keepdims=True)
    return d * lax.rsqrt(var + EPS) * g + b


def _gelu_tanh(x):
    c = 0.7978845608028654
    return 0.5 * x * (1.0 + jnp.tanh(c * (x + 0.044715 * (x * x * x))))


def _head_lane_id(width, per_head):
    return lax.broadcasted_iota(jnp.int32, (1, width), 1) // per_head


def _in_proj_kernel(x_ref, wa_ref, wcqkr_ref, wckv_ref, wc_ref, gcq_ref, gckv_ref,
                    wuqn_ref, wuqr_ref, wukt_ref, mixw_ref, mixb_ref, ga_ref, seg_ref,
                    cos_ref, sin_ref,
                    ya_ref, ql_ref, qr_ref, kcat_ref, sbq_ref, sbk_ref, sbv_ref,
                    ckv_ref, kr_ref, k_ref, v_ref, *gv_refs, tm):
    xb = x_ref[...].astype(BF16)
    grp = _head_lane_id(W_A, DG_A)
    lower_half = (lax.broadcasted_iota(jnp.int32, (1, 256), 1) % ROPE_DIM) < (ROPE_DIM // 2)
    cos = cos_ref[...]
    sin = sin_ref[...]

    def seg_mean(val):
        return _dot(val.astype(BF16), seg_ref[...]) * (1.0 / DG_A)

    def rope(val):
        w = val.shape[1]
        swapped = jnp.where(lower_half[:, :w], pltpu.roll(val, w - ROPE_DIM // 2, 1),
                            pltpu.roll(val, ROPE_DIM // 2, 1))
        return val * cos[:, :w] + swapped * sin[:, :w]

    pa = _dot(xb, wa_ref[...])
    u = _gelu_tanh(pa[:, :W_A])
    gv = _gelu_tanh(pa[:, W_A:])
    mu = seg_mean(gv)
    dv = gv - mu
    v = dv * lax.rsqrt(seg_mean(dv * dv) + EPS)
    if gv_refs:
        gv_refs[0][...] = v
    vb = v.astype(BF16)
    zero_b = jnp.zeros((GMLP_CHUNK, W_A), BF16)
    mixed = []
    for c in range(tm // GMLP_CHUNK):
        vc = vb[c * GMLP_CHUNK:(c + 1) * GMLP_CHUNK, :]
        stacked = jnp.concatenate([jnp.where(grp == g, vc, zero_b) for g in range(G_A)], axis=0)
        mixed.append(_dot(mixw_ref[...], stacked) + mixb_ref[...])
    y_a = u * jnp.concatenate(mixed, axis=0)
    ya_ref[...] = _rms(y_a, ga_ref[...]).astype(BF16)

    pq = _dot(xb, wcqkr_ref[...])
    cqn = _rms(pq[:, :Q_LORA], gcq_ref[...]).astype(BF16)
    qn = _dot(cqn, wuqn_ref[...])
    qr = rope(_dot(cqn, wuqr_ref[...]))
    qr_ref[...] = (qr * MLA_QSCALE).astype(BF16)
    for h in range(H_B):
        ql = _dot(qn[:, h * NOPE_DIM:(h + 1) * NOPE_DIM].astype(BF16), wukt_ref[h])
        ql_ref[:, h * KV_LORA:(h + 1) * KV_LORA] = (ql * MLA_QSCALE).astype(BF16)
    ckv = _rms(_dot(xb, wckv_ref[...]), gckv_ref[...])
    ckv_ref[...] = ckv
    kr2 = rope(pq[:, Q_LORA:])
    kr_ref[...] = kr2[:, :ROPE_DIM]
    kcat_ref[:, :KV_LORA] = ckv.astype(BF16)
    kr2b = kr2.astype(BF16)
    kcat_ref[:, KV_LORA:KV_LORA + LANES] = kr2b
    kcat_ref[:, KV_LORA + LANES:] = kr2b

    pc = _dot(xb, wc_ref[...])
    sbq_ref[...] = (pc[:, :W_C] * SB_QSCALE).astype(BF16)
    kc = pc[:, W_C:2 * W_C]
    vc_ = pc[:, 2 * W_C:]
    k_ref[...] = kc
    v_ref[...] = vc_
    sbk_ref[...] = kc.astype(BF16)
    sbv_ref[...] = vc_.astype(BF16)


def _in_proj(x2d, lw, cos_tab, sin_tab, *, tm, want_gv):
    t = x2d.shape[0]
    n_pos_blocks = cos_tab.shape[0] // tm
    tok = lambda w: pl.BlockSpec((tm, w), lambda i: (i, 0))
    full = lambda a: pl.BlockSpec(a.shape, lambda i: (0,) * a.ndim)
    pos = pl.BlockSpec((tm, 256), lambda i: (i % n_pos_blocks, 0))
    weights = (lw['w_a'], lw['w_cqkr'], lw['w_ckv'], lw['w_c'], lw['g_cq'], lw['g_ckv'],
               lw['w_uqn'], lw['w_uqr'], lw['w_ukt'], lw['mixw'], lw['mixb'], lw['g_a'], lw['seg'])
    out_shape = [
        jax.ShapeDtypeStruct((t, W_A), BF16),
        jax.ShapeDtypeStruct((t, H_B * KV_LORA), BF16),
        jax.ShapeDtypeStruct((t, H_B * ROPE_DIM), BF16),
        jax.ShapeDtypeStruct((t, 2 * KV_LORA), BF16),
        jax.ShapeDtypeStruct((t, W_C), BF16),
        jax.ShapeDtypeStruct((t, W_C), BF16),
        jax.ShapeDtypeStruct((t, W_C), BF16),
        jax.ShapeDtypeStruct((t, KV_LORA), F32),
        jax.ShapeDtypeStruct((t, ROPE_DIM), F32),
        jax.ShapeDtypeStruct((t, W_C), F32),
        jax.ShapeDtypeStruct((t, W_C), F32),
    ]
    if want_gv:
        out_shape.append(jax.ShapeDtypeStruct((t, W_A), F32))
    return pl.pallas_call(
        functools.partial(_in_proj_kernel, tm=tm),
        grid=(t // tm,),
        in_specs=[tok(x2d.shape[1])] + [full(w) for w in weights] + [pos, pos],
        out_specs=[tok(s.shape[1]) for s in out_shape],
        out_shape=out_shape,
        compiler_params=pltpu.CompilerParams(dimension_semantics=("parallel",),
                                             vmem_limit_bytes=VMEM_LIMIT_BYTES),
        name="in_proj",
    )(x2d, *weights, cos_tab, sin_tab)


def _mla_finish(acc, l_inv, wuv_ref, gb_ref, rows):
    out = acc * _rep(l_inv, KV_LORA // LANES)
    ys = [_dot(out[h * rows:(h + 1) * rows].astype(BF16), wuv_ref[h]) for h in range(H_B)]
    return _rms(jnp.concatenate(ys, axis=1), gb_ref[...]).astype(BF16)


def _mla_kernel(ql_ref, qr_ref, kcat_ref, hide_ref, wuv_ref, gb_ref, o_ref, qs_ref, sa_ref, sb_ref,
                m_ref, l_ref, acc_ref, *, tq):
    qi = pl.program_id(1)
    head = _head_lane_id(H_B * ROPE_DIM, ROPE_DIM)
    qr = qr_ref[0]
    for h in range(H_B):
        qs_ref[h * tq:(h + 1) * tq, :KV_LORA] = ql_ref[0, :, h * KV_LORA:(h + 1) * KV_LORA]
        qs_ref[h * tq:(h + 1) * tq, KV_LORA:] = jnp.where(head == h, qr, jnp.zeros_like(qr))

    m_ref[...] = jnp.full(m_ref.shape, NEG_BIG, F32)
    l_ref[...] = jnp.zeros(l_ref.shape, F32)
    acc_ref[...] = jnp.zeros(acc_ref.shape, F32)

    def key_block(j):
        return kcat_ref[0, pl.ds(pl.multiple_of(j * tq, tq), tq), :]

    def scores_into(dst_ref, j):
        dst_ref[...] = _dot_nt(qs_ref[...], key_block(j))

    def consume(src_ref, j, diagonal):
        s = src_ref[...]
        if diagonal:
            s = s + hide_ref[...]
        m_prev = m_ref[...]
        m_next = jnp.maximum(m_prev, jnp.max(s, axis=1, keepdims=True))
        p = jnp.exp2(s - _rep(m_next, tq // LANES))
        alpha = jnp.exp2(m_prev - m_next)
        l_ref[...] = alpha * l_ref[...] + jnp.sum(p, axis=1, keepdims=True)
        m_ref[...] = m_next
        acc_ref[...] = (acc_ref[...] * _rep(alpha, KV_LORA // LANES)
                        + _dot(p.astype(BF16), key_block(j)[:, :KV_LORA]))

    scores_into(sa_ref, 0)

    def pair(t, carry):
        scores_into(sb_ref, 2 * t + 1)
        consume(sa_ref, 2 * t, False)
        scores_into(sa_ref, 2 * t + 2)
        consume(sb_ref, 2 * t + 1, False)
        return carry

    lax.fori_loop(0, qi // 2, pair, 0)

    @pl.when(qi % 2 == 0)
    def _():
        consume(sa_ref, qi, True)

    @pl.when(qi % 2 == 1)
    def _():
        scores_into(sb_ref, qi)
        consume(sa_ref, qi - 1, False)
        consume(sb_ref, qi, True)

    o_ref[0] = _mla_finish(acc_ref[...], 1.0 / l_ref[...], wuv_ref, gb_ref, tq)


def _mla_prompt(ql, qr, kcat, w_uv, g_b, *, tq):
    b, s, _ = ql.shape
    rows = H_B * tq
    row = lax.broadcasted_iota(jnp.int32, (rows, tq), 0) % tq
    col = lax.broadcasted_iota(jnp.int32, (rows, tq), 1)
    hide = jnp.where((col // CHUNK) <= (row // CHUNK), 0.0, NEG_BIG).astype(F32)
    return pl.pallas_call(
        functools.partial(_mla_kernel, tq=tq),
        grid=(b, s // tq),
        in_specs=[
            pl.BlockSpec((1, tq, H_B * KV_LORA), lambda bi, qi: (bi, qi, 0)),
            pl.BlockSpec((1, tq, H_B * ROPE_DIM), lambda bi, qi: (bi, qi, 0)),
            pl.BlockSpec((1, s, 2 * KV_LORA), lambda bi, qi: (bi, 0, 0)),
            pl.BlockSpec(hide.shape, lambda bi, qi: (0, 0)),
            pl.BlockSpec(w_uv.shape, lambda bi, qi: (0, 0, 0)),
            pl.BlockSpec(g_b.shape, lambda bi, qi: (0, 0)),
        ],
        out_specs=pl.BlockSpec((1, tq, W_B), lambda bi, qi: (bi, qi, 0)),
        out_shape=jax.ShapeDtypeStruct((b, s, W_B), BF16),
        scratch_shapes=[pltpu.VMEM((rows, 2 * KV_LORA), BF16),
                        pltpu.VMEM((rows, tq), F32),
                        pltpu.VMEM((rows, tq), F32),
                        pltpu.VMEM((rows, LANES), F32),
                        pltpu.VMEM((rows, LANES), F32),
                        pltpu.VMEM((rows, KV_LORA), F32)],
        compiler_params=pltpu.CompilerParams(dimension_semantics=("parallel", "arbitrary"),
                                             vmem_limit_bytes=VMEM_LIMIT_BYTES),
        name="mla_prompt",
    )(ql, qr, kcat, hide, w_uv, g_b)


def _sb_block(z, vb, tri, carry, vis01=None, hide=None):
    neg_abs = lax.bitcast_convert_type(
        lax.bitcast_convert_type(z, jnp.uint32) | jnp.uint32(0x80000000), F32)
    skip = jnp.maximum(z, 0.0) + jnp.log(1.0 + jnp.exp2(neg_abs)) * LOG2E
    log_take = z - skip
    if vis01 is not None:
        skip = skip * vis01
    later = _dot(skip.astype(BF16), tri)
    arg = log_take - later - _rep(carry, z.shape[1] // LANES)
    if hide is not None:
        arg = arg + hide
    w = jnp.exp2(arg)
    return _dot(w.astype(BF16), vb), carry + jnp.sum(skip, axis=1, keepdims=True)


def _strict_causal_masks(rows_per_head, n_rows, n_cols):
    row = lax.broadcasted_iota(jnp.int32, (n_rows, n_cols), 0) % rows_per_head
    col = lax.broadcasted_iota(jnp.int32, (n_rows, n_cols), 1)
    vis = col < row
    return jnp.where(vis, 1.0, 0.0).astype(F32), jnp.where(vis, 0.0, NEG_BIG).astype(F32)


def _stack_heads_masked(q, per_head):
    head = _head_lane_id(q.shape[1], per_head)
    return jnp.concatenate([jnp.where(head == h, q, jnp.zeros_like(q)) for h in range(H_C)], axis=0)


def _unstack_heads(acc, rows, per_head):
    head = _head_lane_id(acc.shape[1], per_head)
    y = jnp.zeros((rows, acc.shape[1]), F32)
    for h in range(H_C):
        y = y + jnp.where(head == h, acc[h * rows:(h + 1) * rows], 0.0)
    return y


def _sb_kernel(q_ref, k_ref, v_ref, tri_ref, vis_ref, hide_ref, gc_ref, o_ref, qs_ref, za_ref,
               zb_ref, c_ref, acc_ref, *, tq):
    qi = pl.program_id(1)
    qs_ref[...] = _stack_heads_masked(q_ref[0], D_C)

    def block_rows(j):
        return pl.ds(pl.multiple_of(j * tq, tq), tq)

    def logits_into(dst_ref, j):
        dst_ref[...] = _dot_nt(qs_ref[...], k_ref[0, block_rows(j), :])

    def consume(src_ref, j, diagonal=False):
        masks = (vis_ref[...], hide_ref[...]) if diagonal else ()
        carry_in = jnp.zeros(c_ref.shape, F32) if diagonal else c_ref[...]
        pv, carry = _sb_block(src_ref[...], v_ref[0, block_rows(j), :], tri_ref[...], carry_in,
                              *masks)
        acc_ref[...] = pv if diagonal else acc_ref[...] + pv
        c_ref[...] = carry

    logits_into(za_ref, qi)

    @pl.when(qi == 0)
    def _():
        consume(za_ref, qi, diagonal=True)

    @pl.when(qi > 0)
    def _():
        logits_into(zb_ref, qi - 1)
        consume(za_ref, qi, diagonal=True)

        def pair(t, carry):
            j = qi - 1 - 2 * t
            logits_into(za_ref, j - 1)
            consume(zb_ref, j)
            logits_into(zb_ref, j - 2)
            consume(za_ref, j - 1)
            return carry

        lax.fori_loop(0, (qi - 1) // 2, pair, 0)

        @pl.when(qi % 2 == 1)
        def _():
            consume(zb_ref, 0)

        @pl.when(qi % 2 == 0)
        def _():
            logits_into(za_ref, 0)
            consume(zb_ref, 1)
            consume(za_ref, 0)

    o_ref[0] = _rms(_unstack_heads(acc_ref[...], tq, D_C), gc_ref[...]).astype(BF16)


def _sb_prompt(q, k, v, tri, g_c, *, tq):
    b, s, _ = q.shape
    rows = H_C * tq
    vis01, hide = _strict_causal_masks(tq, rows, tq)
    return pl.pallas_call(
        functools.partial(_sb_kernel, tq=tq),
        grid=(b, s // tq),
        in_specs=[
            pl.BlockSpec((1, tq, W_C), lambda bi, qi: (bi, qi, 0)),
            pl.BlockSpec((1, s, W_C), lambda bi, qi: (bi, 0, 0)),
            pl.BlockSpec((1, s, W_C), lambda bi, qi: (bi, 0, 0)),
            pl.BlockSpec(tri.shape, lambda bi, qi: (0, 0)),
            pl.BlockSpec(vis01.shape, lambda bi, qi: (0, 0)),
            pl.BlockSpec(hide.shape, lambda bi, qi: (0, 0)),
            pl.BlockSpec(g_c.shape, lambda bi, qi: (0, 0)),
        ],
        out_specs=pl.BlockSpec((1, tq, W_C), lambda bi, qi: (bi, qi, 0)),
        out_shape=jax.ShapeDtypeStruct((b, s, W_C), BF16),
        scratch_shapes=[pltpu.VMEM((rows, W_C), BF16),
                        pltpu.VMEM((rows, tq), F32),
                        pltpu.VMEM((rows, tq), F32),
                        pltpu.VMEM((rows, LANES), F32),
                        pltpu.VMEM((rows, W_C), F32)],
        compiler_params=pltpu.CompilerParams(dimension_semantics=("parallel", "arbitrary"),
                                             vmem_limit_bytes=VMEM_LIMIT_BYTES),
        name="sb_prompt",
    )(q, k, v, tri, vis01, hide, g_c)


def _mla_dec_kernel(ql_ref, qr_ref, ckv_p_ref, kr_p_ref, ckv_n_ref, kr_n_ref, wuv_ref, gb_ref,
                    o_ref, m_ref, l_ref, acc_ref, *, n_new):
    kb_i = pl.program_id(1)

    @pl.when(kb_i == 0)
    def _():
        m_ref[...] = jnp.full(m_ref.shape, NEG_BIG, F32)
        l_ref[...] = jnp.zeros(l_ref.shape, F32)
        acc_ref[...] = jnp.zeros(acc_ref.shape, F32)

    def update(ckv_f32, kr_f32, n_valid):
        ckv_b = ckv_f32.astype(BF16)
        s = _dot_nt(ql_ref[0], ckv_b) + _dot_nt(qr_ref[0], kr_f32.astype(BF16))
        if n_valid is not None:
            s = jnp.where(lax.broadcasted_iota(jnp.int32, s.shape, 1) < n_valid, s, NEG_BIG)
        m_prev = m_ref[...]
        m_next = jnp.maximum(m_prev, jnp.max(s, axis=1, keepdims=True))
        p = jnp.exp2(s - _rep(m_next, s.shape[1] // LANES))
        alpha = jnp.exp2(m_prev - m_next)
        l_ref[...] = alpha * l_ref[...] + jnp.sum(p, axis=1, keepdims=True)
        m_ref[...] = m_next
        acc_ref[...] = acc_ref[...] * _rep(alpha, KV_LORA // LANES) + _dot(p.astype(BF16), ckv_b)

    update(ckv_p_ref[...], kr_p_ref[...], None)

    @pl.when(kb_i == pl.num_programs(1) - 1)
    def _():
        update(ckv_n_ref[0], kr_n_ref[0], n_new)
        o_ref[0] = _mla_finish(acc_ref[...], 1.0 / l_ref[...], wuv_ref, gb_ref, n_new)


def _mla_decode(ql_s, qr_s, cache_ckv, cache_kr, layer, ckv_new, kr_new, w_uv, g_b, *, tk):
    b, rows, _ = ql_s.shape
    n_new = rows // H_B
    past = cache_ckv.shape[2]
    return pl.pallas_call(
        functools.partial(_mla_dec_kernel, n_new=n_new),
        grid=(b, past // tk),
        in_specs=[
            pl.BlockSpec((1, rows, KV_LORA), lambda bi, ki: (bi, 0, 0)),
            pl.BlockSpec((1, rows, ROPE_DIM), lambda bi, ki: (bi, 0, 0)),
            pl.BlockSpec((None, None, tk, KV_LORA), lambda bi, ki: (layer, bi, ki, 0)),
            pl.BlockSpec((None, None, tk, ROPE_DIM), lambda bi, ki: (layer, bi, ki, 0)),
            pl.BlockSpec((1,) + ckv_new.shape[1:], lambda bi, ki: (bi, 0, 0)),
            pl.BlockSpec((1,) + kr_new.shape[1:], lambda bi, ki: (bi, 0, 0)),
            pl.BlockSpec(w_uv.shape, lambda bi, ki: (0, 0, 0)),
            pl.BlockSpec(g_b.shape, lambda bi, ki: (0, 0)),
        ],
        out_specs=pl.BlockSpec((1, n_new, W_B), lambda bi, ki: (bi, 0, 0)),
        out_shape=jax.ShapeDtypeStruct((b, n_new, W_B), BF16),
        scratch_shapes=[pltpu.VMEM((rows, LANES), F32),
                        pltpu.VMEM((rows, LANES), F32),
                        pltpu.VMEM((rows, KV_LORA), F32)],
        compiler_params=pltpu.CompilerParams(dimension_semantics=("parallel", "arbitrary"),
                                             vmem_limit_bytes=VMEM_LIMIT_BYTES),
        name="mla_decode",
    )(ql_s, qr_s, cache_ckv, cache_kr, ckv_new, kr_new, w_uv, g_b)


def _sb_dec_kernel(q_ref, k_p_ref, v_p_ref, k_n_ref, v_n_ref, tri_p_ref, tri_n_ref, gc_ref,
                   o_ref, qs_ref, c_ref, acc_ref, *, n_new):
    kb_i = pl.program_id(1)

    @pl.when(kb_i == 0)
    def _():
        qs_ref[...] = _stack_heads_masked(q_ref[...], D_C)
        vis01, hide = _strict_causal_masks(n_new, H_C * n_new, k_n_ref.shape[1])
        pv, carry = _sb_block(_dot_nt(qs_ref[...], k_n_ref[0]), v_n_ref[0], tri_n_ref[...],
                              jnp.zeros(c_ref.shape, F32), vis01, hide)
        acc_ref[...] = pv
        c_ref[...] = carry

    pv, carry = _sb_block(_dot_nt(qs_ref[...], k_p_ref[...].astype(BF16)), v_p_ref[...].astype(BF16),
                          tri_p_ref[...], c_ref[...], None)
    acc_ref[...] += pv
    c_ref[...] = carry

    @pl.when(kb_i == pl.num_programs(1) - 1)
    def _():
        o_ref[...] = _rms(_unstack_heads(acc_ref[...], n_new, D_C), gc_ref[...]).astype(BF16)


def _sb_decode(q, cache_k, cache_v, layer, k_new, v_new, tri_p, tri_n, g_c, *, n_new, tk):
    t = q.shape[0]
    b = t // n_new
    nkb = cache_k.shape[2] // tk
    rows = H_C * n_new
    return pl.pallas_call(
        functools.partial(_sb_dec_kernel, n_new=n_new),
        grid=(b, nkb),
        in_specs=[
            pl.BlockSpec((n_new, W_C), lambda bi, ki: (bi, 0)),
            pl.BlockSpec((None, None, tk, W_C), lambda bi, ki: (layer, bi, nkb - 1 - ki, 0)),
            pl.BlockSpec((None, None, tk, W_C), lambda bi, ki: (layer, bi, nkb - 1 - ki, 0)),
            pl.BlockSpec((1,) + k_new.shape[1:], lambda bi, ki: (bi, 0, 0)),
            pl.BlockSpec((1,) + v_new.shape[1:], lambda bi, ki: (bi, 0, 0)),
            pl.BlockSpec(tri_p.shape, lambda bi, ki: (0, 0)),
            pl.BlockSpec(tri_n.shape, lambda bi, ki: (0, 0)),
            pl.BlockSpec(g_c.shape, lambda bi, ki: (0, 0)),
        ],
        out_specs=pl.BlockSpec((n_new, W_C), lambda bi, ki: (bi, 0)),
        out_shape=jax.ShapeDtypeStruct((t, W_C), BF16),
        scratch_shapes=[pltpu.VMEM((rows, W_C), BF16),
                        pltpu.VMEM((rows, LANES), F32),
                        pltpu.VMEM((rows, W_C), F32)],
        compiler_params=pltpu.CompilerParams(dimension_semantics=("parallel", "arbitrary"),
                                             vmem_limit_bytes=VMEM_LIMIT_BYTES),
        name="sb_decode",
    )(q, cache_k, cache_v, k_new, v_new, tri_p, tri_n, g_c)


def _out_mlp_kernel(x_ref, ya_ref, yb_ref, yc_ref, wout_ref, g1_ref, b1_ref, wup_ref, bup_ref,
                    wdown_ref, bdown_ref, g2_ref, b2_ref, o_ref, *, alpha, ff_chunk):
    y = jnp.concatenate([ya_ref[...], yb_ref[...], yc_ref[...]], axis=1)
    x1 = _layer_norm(alpha * x_ref[...] + _dot(y, wout_ref[...]), g1_ref[...], b1_ref[...])
    x1b = x1.astype(BF16)
    h = jnp.zeros(x1.shape, F32)
    for c in range(wup_ref.shape[1] // ff_chunk):
        sl = slice(c * ff_chunk, (c + 1) * ff_chunk)
        a = jnp.maximum(_dot(x1b, wup_ref[:, sl]) + bup_ref[:, sl], 0.0)
        h = h + _dot((a * a).astype(BF16), wdown_ref[sl, :])
    o_ref[...] = _layer_norm(alpha * x1 + h + bdown_ref[...], g2_ref[...], b2_ref[...])


def _out_mlp(x2d, ya, yb, yc, lw, *, tm, alpha):
    t, d = x2d.shape
    tok = lambda w: pl.BlockSpec((tm, w), lambda i: (i, 0))
    const = lambda a: pl.BlockSpec(a.shape, lambda i: (0,) * a.ndim, pipeline_mode=pl.Buffered(1))
    weights = (lw['w_out'], lw['ln1_g'], lw['ln1_b'], lw['w_up'], lw['b_up'], lw['w_down'],
               lw['b_down'], lw['ln2_g'], lw['ln2_b'])
    return pl.pallas_call(
        functools.partial(_out_mlp_kernel, alpha=alpha, ff_chunk=1024),
        grid=(t // tm,),
        in_specs=[tok(d), tok(W_A), tok(W_B), tok(W_C)] + [const(w) for w in weights],
        out_specs=tok(d),
        out_shape=jax.ShapeDtypeStruct((t, d), F32),
        compiler_params=pltpu.CompilerParams(dimension_semantics=("parallel",),
                                             vmem_limit_bytes=VMEM_LIMIT_BYTES),
        name="out_mlp",
    )(x2d, ya, yb, yc, *weights)


def _rope_tables(pos):
    half = ROPE_DIM // 2
    inv = ROPE_THETA ** (-jnp.arange(half, dtype=F32) / half)
    ang = pos.astype(F32)[:, None] * inv[None, :]
    cos, sin = jnp.cos(ang), jnp.sin(ang)
    cos_t = jnp.tile(jnp.concatenate([cos, cos], -1), (1, H_B))
    sin_t = jnp.tile(jnp.concatenate([-sin, sin], -1), (1, H_B))
    return cos_t, sin_t


def _strict_lower_ones(n):
    j = lax.broadcasted_iota(jnp.int32, (n, n), 0)
    s = lax.broadcasted_iota(jnp.int32, (n, n), 1)
    return (j > s).astype(BF16)


def _row(a):
    return a.reshape(1, -1).astype(F32)


def _prep_layer(l, w_in, w_s, b_s, g_cq, g_ckv, w_uq, w_uk, w_uv, g_mix, w_out, ln1_g, ln1_b,
                w_up, b_up, w_down, b_down, ln2_g, ln2_b, n_dec):
    wi = w_in[l].astype(BF16)
    o_cq, o_ckv, o_kr, o_c = 2 * W_A, 2 * W_A + Q_LORA, 2 * W_A + Q_LORA + KV_LORA, \
        2 * W_A + Q_LORA + KV_LORA + ROPE_DIM
    uq = w_uq[l].astype(BF16)
    lw = {
        'w_a': wi[:, :o_cq], 'w_ckv': wi[:, o_ckv:o_kr], 'w_c': wi[:, o_c:],
        'w_cqkr': jnp.concatenate([wi[:, o_cq:o_ckv], wi[:, o_kr:o_c], wi[:, o_kr:o_c]], axis=1),
        'g_cq': _row(g_cq[l]), 'g_ckv': _row(g_ckv[l]),
        'w_uqn': uq[:, :, :NOPE_DIM].reshape(Q_LORA, H_B * NOPE_DIM),
        'w_uqr': uq[:, :, NOPE_DIM:].reshape(Q_LORA, H_B * ROPE_DIM),
        'w_ukt': jnp.swapaxes(w_uk[l], 1, 2).astype(BF16),
        'w_uv': w_uv[l].astype(BF16),
        'g_a': _row(g_mix[l, :W_A]), 'g_b': _row(g_mix[l, W_A:W_A + W_B]),
        'g_c': _row(g_mix[l, W_A + W_B:]),
        'w_out': w_out[l].astype(BF16), 'ln1_g': _row(ln1_g[l]), 'ln1_b': _row(ln1_b[l]),
        'w_up': w_up[l].astype(BF16), 'b_up': _row(b_up[l]), 'w_down': w_down[l].astype(BF16),
        'b_down': _row(b_down[l]), 'ln2_g': _row(ln2_g[l]), 'ln2_b': _row(ln2_b[l]),
    }
    seg = lax.broadcasted_iota(jnp.int32, (W_A, W_A), 0) // DG_A
    lw['seg'] = (seg == seg.T).astype(BF16)
    idx = jnp.arange(GMLP_CHUNK)
    vis = (idx[None, :] // CHUNK) <= (idx[:, None] // CHUNK)
    wm = jnp.where(vis[None], w_s[l], 0.0)
    lw['mixw'] = jnp.transpose(wm, (1, 0, 2)).reshape(GMLP_CHUNK, G_A * GMLP_CHUNK).astype(BF16)
    lw['mixb'] = jnp.repeat(b_s[l].T, DG_A, axis=1).astype(F32)
    reps = GMLP_CHUNK // n_dec
    wd = w_s[l][:, :n_dec, :n_dec]
    wmd = jax.vmap(lambda w: jnp.kron(jnp.eye(reps, dtype=F32), w))(wd)
    lw_dec = dict(lw)
    lw_dec['mixw'] = jnp.transpose(wmd, (1, 0, 2)).reshape(GMLP_CHUNK, G_A * GMLP_CHUNK).astype(BF16)
    lw_dec['mixb'] = jnp.repeat(jnp.tile(b_s[l][:, :n_dec].T, (reps, 1)), DG_A, axis=1).astype(F32)
    return lw, lw_dec


def kernel(x_prompt, x_sample, cache_mla_ckv, cache_mla_krope, cache_sb_k, cache_sb_v, w_in, w_s, b_s, g_cq, g_ckv, w_uq, w_uk, w_uv, g_mix, w_out, ln1_g, ln1_b, w_up, b_up, w_down, b_down, ln2_g, ln2_b):
    b, s, d = x_prompt.shape
    db, ds, _ = x_sample.shape
    depth = w_in.shape[0]
    past = cache_mla_ckv.shape[2]
    alpha = (2 * depth) ** 0.25
    tm_p, tq = 512, 256
    t_dec = db * ds
    assert t_dec == GMLP_CHUNK and s % tm_p == 0 and past % CHUNK == 0 and ds <= CHUNK

    cos_p, sin_p = _rope_tables(jnp.arange(s, dtype=jnp.int32))
    cos_s, sin_s = _rope_tables(jnp.tile(past + jnp.arange(ds, dtype=jnp.int32), db))
    tri_p = _strict_lower_ones(tq)
    tk_dec = 1024
    tri_dp = _strict_lower_ones(tk_dec)
    tri_dn = _strict_lower_ones(LANES)
    cache_k2 = cache_sb_k.reshape(depth, db, past, W_C)
    cache_v2 = cache_sb_v.reshape(depth, db, past, W_C)

    xp = x_prompt.reshape(b * s, d)
    xs = x_sample.reshape(t_dec, d)
    st_p = [[] for _ in range(4)]
    st_s = [[] for _ in range(5)]
    for l in range(depth):
        lw, lw_dec = _prep_layer(l, w_in, w_s, b_s, g_cq, g_ckv, w_uq, w_uk, w_uv, g_mix, w_out,
                                 ln1_g, ln1_b, w_up, b_up, w_down, b_down, ln2_g, ln2_b, ds)
        ya, ql, qr, kcat, sbq, sbk, sbv, ckv, kr, kk, vv = _in_proj(
            xp, lw, cos_p, sin_p, tm=tm_p, want_gv=False)
        r3 = lambda a: a.reshape(b, s, a.shape[-1])
        yb = _mla_prompt(r3(ql), r3(qr), r3(kcat), lw['w_uv'], lw['g_b'], tq=tq)
        yc = _sb_prompt(r3(sbq), r3(sbk), r3(sbv), tri_p, lw['g_c'], tq=tq)
        xp = _out_mlp(xp, ya, yb.reshape(b * s, W_B), yc.reshape(b * s, W_C), lw, tm=tm_p, alpha=alpha)
        for lst, a in zip(st_p, (ckv, kr, kk, vv)):
            lst.append(a)
        ya, ql, qr, kcat, sbq, sbk, sbv, ckv, kr, kk, vv, gv = _in_proj(
            xs, lw_dec, cos_s, sin_s, tm=t_dec, want_gv=True)
        ql_s = ql.reshape(db, ds, H_B, KV_LORA).transpose(0, 2, 1, 3).reshape(db, H_B * ds, KV_LORA)
        qr_s = qr.reshape(db, ds, H_B, ROPE_DIM).transpose(0, 2, 1, 3).reshape(db, H_B * ds, ROPE_DIM)
        pad_new = lambda a: jnp.pad(a.reshape(db, ds, a.shape[-1]), ((0, 0), (0, LANES - ds), (0, 0)))
        yb = _mla_decode(ql_s, qr_s, cache_mla_ckv, cache_mla_krope, l, pad_new(ckv), pad_new(kr),
                         lw['w_uv'], lw['g_b'], tk=tk_dec)
        yc = _sb_decode(sbq, cache_k2, cache_v2, l, pad_new(sbk), pad_new(sbv), tri_dp, tri_dn,
                        lw['g_c'], n_new=ds, tk=tk_dec)
        xs = _out_mlp(xs, ya, yb.reshape(t_dec, W_B), yc, lw, tm=t_dec, alpha=alpha)
        for lst, a in zip(st_s, (ckv, kr, kk, vv, gv)):
            lst.append(a)

    def stack_p(lst, tail):
        return jnp.stack(lst).reshape((depth, b, s) + tail)

    def stack_s(lst, tail):
        return jnp.stack(lst).reshape((depth, db, ds) + tail)

    return (xp.reshape(b, s, d), xs.reshape(db, ds, d),
            stack_p(st_p[0], (KV_LORA,)), stack_p(st_p[1], (ROPE_DIM,)),
            stack_p(st_p[2], (H_C, D_C)), stack_p(st_p[3], (H_C, D_C)),
            stack_s(st_s[0], (KV_LORA,)), stack_s(st_s[1], (ROPE_DIM,)),
            stack_s(st_s[2], (H_C, D_C)), stack_s(st_s[3], (H_C, D_C)),
            stack_s(st_s[4], (G_A, DG_A)))
```

```python
import functools

import jax
import jax.numpy as jnp
from jax import lax
from jax.experimental import pallas as pl
from jax.experimental.pallas import tpu as pltpu

F32 = jnp.float32
BF16 = jnp.bfloat16

CHUNK = 64
GMLP_CHUNK = 128
G_A, DG_A = 4, 64
W_A = G_A * DG_A
H_B, NOPE_DIM, ROPE_DIM, V_DIM = 4, 128, 64, 128
W_B = H_B * V_DIM
Q_LORA, KV_LORA = 384, 256
ROPE_THETA = 10000.0
LOG2E = 1.4426950408889634
MLA_SCALE = (NOPE_DIM + ROPE_DIM) ** -0.5
MLA_QSCALE = MLA_SCALE * LOG2E
H_C, D_C = 4, 64
W_C = H_C * D_C
SB_SCALE = D_C ** -0.5
SB_QSCALE = SB_SCALE * LOG2E
EPS = 1e-5
NEG_BIG = -1e30

LANES = 128
VMEM_LIMIT_BYTES = 56 * 1024 * 1024


def _dot(a, b):
    return jnp.dot(a, b, preferred_element_type=F32)


def _dot_nt(a, b):
    return lax.dot_general(a, b, (((1,), (1,)), ((), ())), preferred_element_type=F32)


def _rep(x, n):
    return x if n == 1 else jnp.concatenate([x] * n, axis=1)


def _rms(x, g):
    ms = jnp.mean(x * x, axis=-1, keepdims=True)
    return x * lax.rsqrt(ms + EPS) * g


def _layer_norm(x, g, b):
    mu = jnp.mean(x, axis=-1, keepdims=True)
    d = x - mu
    var = jnp.mean(d * d, axis=-1, keepdims=True)
    return d * lax.rsqrt(var + EPS) * g + b


def _gelu_tanh(x):
    c = 0.7978845608028654
    return 0.5 * x * (1.0 + jnp.tanh(c * (x + 0.044715 * (x * x * x))))


def _head_lane_id(width, per_head):
    return lax.broadcasted_iota(jnp.int32, (1, width), 1) // per_head


def _in_proj_kernel(x_ref, wa_ref, wcqkr_ref, wckv_ref, wc_ref, gcq_ref, gckv_ref,
                    wuqn_ref, wuqr_ref, wukt_ref, mixw_ref, mixb_ref, ga_ref, seg_ref,
                    cos_ref, sin_ref,
                    ya_ref, ql_ref, qr_ref, kcat_ref, sbq_ref, sbk_ref, sbv_ref,
                    ckv_ref, kr_ref, k_ref, v_ref, *gv_refs, tm):
    xb = x_ref[...].astype(BF16)
    grp = _head_lane_id(W_A, DG_A)
    lower_half = (lax.broadcasted_iota(jnp.int32, (1, 256), 1) % ROPE_DIM) < (ROPE_DIM // 2)
    cos = cos_ref[...]
    sin = sin_ref[...]

    def seg_mean(val):
        return _dot(val.astype(BF16), seg_ref[...]) * (1.0 / DG_A)

    def rope(val):
        w = val.shape[1]
        swapped = jnp.where(lower_half[:, :w], pltpu.roll(val, w - ROPE_DIM // 2, 1),
                            pltpu.roll(val, ROPE_DIM // 2, 1))
        return val * cos[:, :w] + swapped * sin[:, :w]

    pa = _dot(xb, wa_ref[...])
    u = _gelu_tanh(pa[:, :W_A])
    gv = _gelu_tanh(pa[:, W_A:])
    mu = seg_mean(gv)
    dv = gv - mu
    v = dv * lax.rsqrt(seg_mean(dv * dv) + EPS)
    if gv_refs:
        gv_refs[0][...] = v
    vb = v.astype(BF16)
    zero_b = jnp.zeros((GMLP_CHUNK, W_A), BF16)
    mixed = []
    for c in range(tm // GMLP_CHUNK):
        vc = vb[c * GMLP_CHUNK:(c + 1) * GMLP_CHUNK, :]
        stacked = jnp.concatenate([jnp.where(grp == g, vc, zero_b) for g in range(G_A)], axis=0)
        mixed.append(_dot(mixw_ref[...], stacked) + mixb_ref[...])
    y_a = u * jnp.concatenate(mixed, axis=0)
    ya_ref[...] = _rms(y_a, ga_ref[...]).astype(BF16)

    pq = _dot(xb, wcqkr_ref[...])
    cqn = _rms(pq[:, :Q_LORA], gcq_ref[...]).astype(BF16)
    qn = _dot(cqn, wuqn_ref[...])
    qr = rope(_dot(cqn, wuqr_ref[...]))
    qr_ref[...] = (qr * MLA_QSCALE).astype(BF16)
    for h in range(H_B):
        ql = _dot(qn[:, h * NOPE_DIM:(h + 1) * NOPE_DIM].astype(BF16), wukt_ref[h])
        ql_ref[:, h * KV_LORA:(h + 1) * KV_LORA] = (ql * MLA_QSCALE).astype(BF16)
    ckv = _rms(_dot(xb, wckv_ref[...]), gckv_ref[...])
    ckv_ref[...] = ckv
    kr2 = rope(pq[:, Q_LORA:])
    kr_ref[...] = kr2[:, :ROPE_DIM]
    kcat_ref[:, :KV_LORA] = ckv.astype(BF16)
    kr2b = kr2.astype(BF16)
    kcat_ref[:, KV_LORA:KV_LORA + LANES] = kr2b
    kcat_ref[:, KV_LORA + LANES:] = kr2b

    pc = _dot(xb, wc_ref[...])
    sbq_ref[...] = (pc[:, :W_C] * SB_QSCALE).astype(BF16)
    kc = pc[:, W_C:2 * W_C]
    vc_ = pc[:, 2 * W_C:]
    k_ref[...] = kc
    v_ref[...] = vc_
    sbk_ref[...] = kc.astype(BF16)
    sbv_ref[...] = vc_.astype(BF16)


def _in_proj(x2d, lw, cos_tab, sin_tab, *, tm, want_gv):
    t = x2d.shape[0]
    n_pos_blocks = cos_tab.shape[0] // tm
    tok = lambda w: pl.BlockSpec((tm, w), lambda i: (i, 0))
    full = lambda a: pl.BlockSpec(a.shape, lambda i: (0,) * a.ndim)
    pos = pl.BlockSpec((tm, 256), lambda i: (i % n_pos_blocks, 0))
    weights = (lw['w_a'], lw['w_cqkr'], lw['w_ckv'], lw['w_c'], lw['g_cq'], lw['g_ckv'],
               lw['w_uqn'], lw['w_uqr'], lw['w_ukt'], lw['mixw'], lw['mixb'], lw['g_a'], lw['seg'])
    out_shape = [
        jax.ShapeDtypeStruct((t, W_A), BF16),
        jax.ShapeDtypeStruct((t, H_B * KV_LORA), BF16),
        jax.ShapeDtypeStruct((t, H_B * ROPE_DIM), BF16),
        jax.ShapeDtypeStruct((t, 2 * KV_LORA), BF16),
        jax.ShapeDtypeStruct((t, W_C), BF16),
        jax.ShapeDtypeStruct((t, W_C), BF16),
        jax.ShapeDtypeStruct((t, W_C), BF16),
        jax.ShapeDtypeStruct((t, KV_LORA), F32),
        jax.ShapeDtypeStruct((t, ROPE_DIM), F32),
        jax.ShapeDtypeStruct((t, W_C), F32),
        jax.ShapeDtypeStruct((t, W_C), F32),
    ]
    if want_gv:
        out_shape.append(jax.ShapeDtypeStruct((t, W_A), F32))
    return pl.pallas_call(
        functools.partial(_in_proj_kernel, tm=tm),
        grid=(t // tm,),
        in_specs=[tok(x2d.shape[1])] + [full(w) for w in weights] + [pos, pos],
        out_specs=[tok(s.shape[1]) for s in out_shape],
        out_shape=out_shape,
        compiler_params=pltpu.CompilerParams(dimension_semantics=("parallel",),
                                             vmem_limit_bytes=VMEM_LIMIT_BYTES),
        name="in_proj",
    )(x2d, *weights, cos_tab, sin_tab)


def _mla_finish(acc, l_inv, wuv_ref, gb_ref, rows):
    out = acc * _rep(l_inv, KV_LORA // LANES)
    ys = [_dot(out[h * rows:(h + 1) * rows].astype(BF16), wuv_ref[h]) for h in range(H_B)]
    return _rms(jnp.concatenate(ys, axis=1), gb_ref[...]).astype(BF16)


def _mla_kernel(ql_ref, qr_ref, kcat_ref, hide_ref, wuv_ref, gb_ref, o_ref, qs_ref, sa_ref, sb_ref,
                m_ref, l_ref, acc_ref, *, tq):
    qi = pl.program_id(1)
    head = _head_lane_id(H_B * ROPE_DIM, ROPE_DIM)
    qr = qr_ref[0]
    for h in range(H_B):
        qs_ref[h * tq:(h + 1) * tq, :KV_LORA] = ql_ref[0, :, h * KV_LORA:(h + 1) * KV_LORA]
        qs_ref[h * tq:(h + 1) * tq, KV_LORA:] = jnp.where(head == h, qr, jnp.zeros_like(qr))

    m_ref[...] = jnp.full(m_ref.shape, NEG_BIG, F32)
    l_ref[...] = jnp.zeros(l_ref.shape, F32)
    acc_ref[...] = jnp.zeros(acc_ref.shape, F32)

    def key_block(j):
        return kcat_ref[0, pl.ds(pl.multiple_of(j * tq, tq), tq), :]

    def scores_into(dst_ref, j):
        dst_ref[...] = _dot_nt(qs_ref[...], key_block(j))

    def consume(src_ref, j, diagonal):
        s = src_ref[...]
        if diagonal:
            s = s + hide_ref[...]
        m_prev = m_ref[...]
        m_next = jnp.maximum(m_prev, jnp.max(s, axis=1, keepdims=True))
        p = jnp.exp2(s - _rep(m_next, tq // LANES))
        alpha = jnp.exp2(m_prev - m_next)
        l_ref[...] = alpha * l_ref[...] + jnp.sum(p, axis=1, keepdims=True)
        m_ref[...] = m_next
        acc_ref[...] = (acc_ref[...] * _rep(alpha, KV_LORA // LANES)
                        + _dot(p.astype(BF16), key_block(j)[:, :KV_LORA]))

    scores_into(sa_ref, 0)

    def pair(t, carry):
        scores_into(sb_ref, 2 * t + 1)
        consume(sa_ref, 2 * t, False)
        scores_into(sa_ref, 2 * t + 2)
        consume(sb_ref, 2 * t + 1, False)
        return carry

    lax.fori_loop(0, qi // 2, pair, 0)

    @pl.when(qi % 2 == 0)
    def _():
        consume(sa_ref, qi, True)

    @pl.when(qi % 2 == 1)
    def _():
        scores_into(sb_ref, qi)
        consume(sa_ref, qi - 1, False)
        consume(sb_ref, qi, True)

    o_ref[0] = _mla_finish(acc_ref[...], 1.0 / l_ref[...], wuv_ref, gb_ref, tq)


def _mla_prompt(ql, qr, kcat, w_uv, g_b, *, tq):
    b, s, _ = ql.shape
    rows = H_B * tq
    row = lax.broadcasted_iota(jnp.int32, (rows, tq), 0) % tq
    col = lax.broadcasted_iota(jnp.int32, (rows, tq), 1)
    hide = jnp.where((col // CHUNK) <= (row // CHUNK), 0.0, NEG_BIG).astype(F32)
    return pl.pallas_call(
        functools.partial(_mla_kernel, tq=tq),
        grid=(b, s // tq),
        in_specs=[
            pl.BlockSpec((1, tq, H_B * KV_LORA), lambda bi, qi: (bi, qi, 0)),
            pl.BlockSpec((1, tq, H_B * ROPE_DIM), lambda bi, qi: (bi, qi, 0)),
            pl.BlockSpec((1, s, 2 * KV_LORA), lambda bi, qi: (bi, 0, 0)),
            pl.BlockSpec(hide.shape, lambda bi, qi: (0, 0)),
            pl.BlockSpec(w_uv.shape, lambda bi, qi: (0, 0, 0)),
            pl.BlockSpec(g_b.shape, lambda bi, qi: (0, 0)),
        ],
        out_specs=pl.BlockSpec((1, tq, W_B), lambda bi, qi: (bi, qi, 0)),
        out_shape=jax.ShapeDtypeStruct((b, s, W_B), BF16),
        scratch_shapes=[pltpu.VMEM((rows, 2 * KV_LORA), BF16),
                        pltpu.VMEM((rows, tq), F32),
                        pltpu.VMEM((rows, tq), F32),
                        pltpu.VMEM((rows, LANES), F32),
                        pltpu.VMEM((rows, LANES), F32),
                        pltpu.VMEM((rows, KV_LORA), F32)],
        compiler_params=pltpu.CompilerParams(dimension_semantics=("parallel", "arbitrary"),
                                             vmem_limit_bytes=VMEM_LIMIT_BYTES),
        name="mla_prompt",
    )(ql, qr, kcat, hide, w_uv, g_b)


def _sb_block(z, vb, tri, carry, vis01=None, hide=None, v_is_transposed=False):
    neg_abs = lax.bitcast_convert_type(
        lax.bitcast_convert_type(z, jnp.uint32) | jnp.uint32(0x80000000), F32)
    skip = jnp.maximum(z, 0.0) + jnp.log(1.0 + jnp.exp2(neg_abs)) * LOG2E
    log_take = z - skip
    if vis01 is not None:
        skip = skip * vis01
    later = _dot(skip.astype(BF16), tri)
    arg = log_take - later - _rep(carry, z.shape[1] // LANES)
    if hide is not None:
        arg = arg + hide
    w = jnp.exp2(arg)
    pv = (_dot_nt if v_is_transposed else _dot)(w.astype(BF16), vb)
    return pv, carry + jnp.sum(skip, axis=1, keepdims=True)


def _strict_causal_masks(rows_per_head, n_rows, n_cols):
    row = lax.broadcasted_iota(jnp.int32, (n_rows, n_cols), 0) % rows_per_head
    col = lax.broadcasted_iota(jnp.int32, (n_rows, n_cols), 1)
    vis = col < row
    return jnp.where(vis, 1.0, 0.0).astype(F32), jnp.where(vis, 0.0, NEG_BIG).astype(F32)


def _stack_heads_masked(q, per_head):
    head = _head_lane_id(q.shape[1], per_head)
    return jnp.concatenate([jnp.where(head == h, q, jnp.zeros_like(q)) for h in range(H_C)], axis=0)


def _unstack_heads(acc, rows, per_head):
    head = _head_lane_id(acc.shape[1], per_head)
    y = jnp.zeros((rows, acc.shape[1]), F32)
    for h in range(H_C):
        y = y + jnp.where(head == h, acc[h * rows:(h + 1) * rows], 0.0)
    return y


def _sb_kernel(q_ref, k_ref, v_ref, tri_ref, vis_ref, hide_ref, gc_ref, o_ref, qs_ref, za_ref,
               zb_ref, c_ref, acc_ref, *, tq):
    qi = pl.program_id(1)
    qs_ref[...] = _stack_heads_masked(q_ref[0], D_C)

    def block_rows(j):
        return pl.ds(pl.multiple_of(j * tq, tq), tq)

    def logits_into(dst_ref, j):
        dst_ref[...] = _dot_nt(qs_ref[...], k_ref[0, block_rows(j), :])

    def consume(src_ref, j, diagonal=False):
        masks = (vis_ref[...], hide_ref[...]) if diagonal else ()
        carry_in = jnp.zeros(c_ref.shape, F32) if diagonal else c_ref[...]
        pv, carry = _sb_block(src_ref[...], v_ref[0, block_rows(j), :], tri_ref[...], carry_in,
                              *masks)
        acc_ref[...] = pv if diagonal else acc_ref[...] + pv
        c_ref[...] = carry

    logits_into(za_ref, qi)

    @pl.when(qi == 0)
    def _():
        consume(za_ref, qi, diagonal=True)

    @pl.when(qi > 0)
    def _():
        logits_into(zb_ref, qi - 1)
        consume(za_ref, qi, diagonal=True)

        def pair(t, carry):
            j = qi - 1 - 2 * t
            logits_into(za_ref, j - 1)
            consume(zb_ref, j)
            logits_into(zb_ref, j - 2)
            consume(za_ref, j - 1)
            return carry

        lax.fori_loop(0, (qi - 1) // 2, pair, 0)

        @pl.when(qi % 2 == 1)
        def _():
            consume(zb_ref, 0)

        @pl.when(qi % 2 == 0)
        def _():
            logits_into(za_ref, 0)
            consume(zb_ref, 1)
            consume(za_ref, 0)

    o_ref[0] = _rms(_unstack_heads(acc_ref[...], tq, D_C), gc_ref[...]).astype(BF16)


def _sb_prompt(q, k, v, tri, g_c, *, tq):
    b, s, _ = q.shape
    rows = H_C * tq
    vis01, hide = _strict_causal_masks(tq, rows, tq)
    return pl.pallas_call(
        functools.partial(_sb_kernel, tq=tq),
        grid=(b, s // tq),
        in_specs=[
            pl.BlockSpec((1, tq, W_C), lambda bi, qi: (bi, qi, 0)),
            pl.BlockSpec((1, s, W_C), lambda bi, qi: (bi, 0, 0)),
            pl.BlockSpec((1, s, W_C), lambda bi, qi: (bi, 0, 0)),
            pl.BlockSpec(tri.shape, lambda bi, qi: (0, 0)),
            pl.BlockSpec(vis01.shape, lambda bi, qi: (0, 0)),
            pl.BlockSpec(hide.shape, lambda bi, qi: (0, 0)),
            pl.BlockSpec(g_c.shape, lambda bi, qi: (0, 0)),
        ],
        out_specs=pl.BlockSpec((1, tq, W_C), lambda bi, qi: (bi, qi, 0)),
        out_shape=jax.ShapeDtypeStruct((b, s, W_C), BF16),
        scratch_shapes=[pltpu.VMEM((rows, W_C), BF16),
                        pltpu.VMEM((rows, tq), F32),
                        pltpu.VMEM((rows, tq), F32),
                        pltpu.VMEM((rows, LANES), F32),
                        pltpu.VMEM((rows, W_C), F32)],
        compiler_params=pltpu.CompilerParams(dimension_semantics=("parallel", "arbitrary"),
                                             vmem_limit_bytes=VMEM_LIMIT_BYTES),
        name="sb_prompt",
    )(q, k, v, tri, vis01, hide, g_c)


def _mla_dec_kernel(ql_ref, qr_ref, ckv_p_ref, krt_p_ref, ckv_n_ref, kr_n_ref, wuv_ref, gb_ref,
                    o_ref, m_ref, l_ref, acc_ref, *, n_new):
    kb_i = pl.program_id(1)

    @pl.when(kb_i == 0)
    def _():
        m_ref[...] = jnp.full(m_ref.shape, NEG_BIG, F32)
        l_ref[...] = jnp.zeros(l_ref.shape, F32)
        acc_ref[...] = jnp.zeros(acc_ref.shape, F32)

    def update(ckv_f32, rope_scores, n_valid):
        ckv_b = ckv_f32.astype(BF16)
        s = _dot_nt(ql_ref[0], ckv_b) + rope_scores
        if n_valid is not None:
            s = jnp.where(lax.broadcasted_iota(jnp.int32, s.shape, 1) < n_valid, s, NEG_BIG)
        m_prev = m_ref[...]
        m_next = jnp.maximum(m_prev, jnp.max(s, axis=1, keepdims=True))
        p = jnp.exp2(s - _rep(m_next, s.shape[1] // LANES))
        alpha = jnp.exp2(m_prev - m_next)
        l_ref[...] = alpha * l_ref[...] + jnp.sum(p, axis=1, keepdims=True)
        m_ref[...] = m_next
        acc_ref[...] = acc_ref[...] * _rep(alpha, KV_LORA // LANES) + _dot(p.astype(BF16), ckv_b)

    update(ckv_p_ref[...], _dot(qr_ref[0], krt_p_ref[...].astype(BF16)), None)

    @pl.when(kb_i == pl.num_programs(1) - 1)
    def _():
        update(ckv_n_ref[0], _dot_nt(qr_ref[0], kr_n_ref[0].astype(BF16)), n_new)
        o_ref[0] = _mla_finish(acc_ref[...], 1.0 / l_ref[...], wuv_ref, gb_ref, n_new)


def _mla_decode(ql_s, qr_s, cache_ckv, cache_kr, layer, ckv_new, kr_new, w_uv, g_b, *, tk):
    b, rows, _ = ql_s.shape
    n_new = rows // H_B
    past = cache_ckv.shape[2]
    return pl.pallas_call(
        functools.partial(_mla_dec_kernel, n_new=n_new),
        grid=(b, past // tk),
        in_specs=[
            pl.BlockSpec((1, rows, KV_LORA), lambda bi, ki: (bi, 0, 0)),
            pl.BlockSpec((1, rows, ROPE_DIM), lambda bi, ki: (bi, 0, 0)),
            pl.BlockSpec((None, None, tk, KV_LORA), lambda bi, ki: (layer, bi, ki, 0)),
            pl.BlockSpec((None, None, ROPE_DIM, tk), lambda bi, ki: (layer, bi, 0, ki)),
            pl.BlockSpec((1,) + ckv_new.shape[1:], lambda bi, ki: (bi, 0, 0)),
            pl.BlockSpec((1,) + kr_new.shape[1:], lambda bi, ki: (bi, 0, 0)),
            pl.BlockSpec(w_uv.shape, lambda bi, ki: (0, 0, 0)),
            pl.BlockSpec(g_b.shape, lambda bi, ki: (0, 0)),
        ],
        out_specs=pl.BlockSpec((1, n_new, W_B), lambda bi, ki: (bi, 0, 0)),
        out_shape=jax.ShapeDtypeStruct((b, n_new, W_B), BF16),
        scratch_shapes=[pltpu.VMEM((rows, LANES), F32),
                        pltpu.VMEM((rows, LANES), F32),
                        pltpu.VMEM((rows, KV_LORA), F32)],
        compiler_params=pltpu.CompilerParams(dimension_semantics=("parallel", "arbitrary"),
                                             vmem_limit_bytes=VMEM_LIMIT_BYTES),
        name="mla_decode",
    )(ql_s, qr_s, cache_ckv, cache_kr, ckv_new, kr_new, w_uv, g_b)


def _sb_dec_kernel(q_ref, kt_p_ref, vt_p_ref, k_n_ref, v_n_ref, tri_p_ref, tri_n_ref, gc_ref,
                   o_ref, qs_ref, c_ref, acc_ref, *, n_new):
    kb_i = pl.program_id(1)

    @pl.when(kb_i == 0)
    def _():
        qs_ref[...] = _stack_heads_masked(q_ref[...], D_C)
        vis01, hide = _strict_causal_masks(n_new, H_C * n_new, k_n_ref.shape[1])
        pv, carry = _sb_block(_dot_nt(qs_ref[...], k_n_ref[0]), v_n_ref[0], tri_n_ref[...],
                              jnp.zeros(c_ref.shape, F32), vis01, hide)
        acc_ref[...] = pv
        c_ref[...] = carry

    z = _dot(qs_ref[...], kt_p_ref[...].astype(BF16))
    vt = vt_p_ref[...].astype(BF16)
    sub = tri_p_ref.shape[0]
    acc = acc_ref[...]
    carry = c_ref[...]
    for c in reversed(range(z.shape[1] // sub)):
        cols = slice(c * sub, (c + 1) * sub)
        pv, carry = _sb_block(z[:, cols], vt[:, cols], tri_p_ref[...], carry, v_is_transposed=True)
        acc = acc + pv
    acc_ref[...] = acc
    c_ref[...] = carry

    @pl.when(kb_i == pl.num_programs(1) - 1)
    def _():
        o_ref[...] = _rms(_unstack_heads(acc_ref[...], n_new, D_C), gc_ref[...]).astype(BF16)


def _sb_decode(q, cache_k, cache_v, layer, k_new, v_new, tri_p, tri_n, g_c, *, n_new, tk):
    t = q.shape[0]
    b = t // n_new
    nkb = cache_k.shape[3] // tk
    rows = H_C * n_new
    return pl.pallas_call(
        functools.partial(_sb_dec_kernel, n_new=n_new),
        grid=(b, nkb),
        in_specs=[
            pl.BlockSpec((n_new, W_C), lambda bi, ki: (bi, 0)),
            pl.BlockSpec((None, None, W_C, tk), lambda bi, ki: (layer, bi, 0, nkb - 1 - ki)),
            pl.BlockSpec((None, None, W_C, tk), lambda bi, ki: (layer, bi, 0, nkb - 1 - ki)),
            pl.BlockSpec((1,) + k_new.shape[1:], lambda bi, ki: (bi, 0, 0)),
            pl.BlockSpec((1,) + v_new.shape[1:], lambda bi, ki: (bi, 0, 0)),
            pl.BlockSpec(tri_p.shape, lambda bi, ki: (0, 0)),
            pl.BlockSpec(tri_n.shape, lambda bi, ki: (0, 0)),
            pl.BlockSpec(g_c.shape, lambda bi, ki: (0, 0)),
        ],
        out_specs=pl.BlockSpec((n_new, W_C), lambda bi, ki: (bi, 0)),
        out_shape=jax.ShapeDtypeStruct((t, W_C), BF16),
        scratch_shapes=[pltpu.VMEM((rows, W_C), BF16),
                        pltpu.VMEM((rows, LANES), F32),
                        pltpu.VMEM((rows, W_C), F32)],
        compiler_params=pltpu.CompilerParams(dimension_semantics=("parallel", "arbitrary"),
                                             vmem_limit_bytes=VMEM_LIMIT_BYTES),
        name="sb_decode",
    )(q, cache_k, cache_v, k_new, v_new, tri_p, tri_n, g_c)


def _out_mlp_kernel(x_ref, ya_ref, yb_ref, yc_ref, wout_ref, g1_ref, b1_ref, wup_ref, bup_ref,
                    wdown_ref, bdown_ref, g2_ref, b2_ref, o_ref, *, alpha, ff_chunk):
    y = jnp.concatenate([ya_ref[...], yb_ref[...], yc_ref[...]], axis=1)
    x1 = _layer_norm(alpha * x_ref[...] + _dot(y, wout_ref[...]), g1_ref[...], b1_ref[...])
    x1b = x1.astype(BF16)
    h = jnp.zeros(x1.shape, F32)
    for c in range(wup_ref.shape[1] // ff_chunk):
        sl = slice(c * ff_chunk, (c + 1) * ff_chunk)
        a = jnp.maximum(_dot(x1b, wup_ref[:, sl]) + bup_ref[:, sl], 0.0)
        h = h + _dot((a * a).astype(BF16), wdown_ref[sl, :])
    o_ref[...] = _layer_norm(alpha * x1 + h + bdown_ref[...], g2_ref[...], b2_ref[...])


def _out_mlp(x2d, ya, yb, yc, lw, *, tm, alpha):
    t, d = x2d.shape
    tok = lambda w: pl.BlockSpec((tm, w), lambda i: (i, 0))
    const = lambda a: pl.BlockSpec(a.shape, lambda i: (0,) * a.ndim, pipeline_mode=pl.Buffered(1))
    weights = (lw['w_out'], lw['ln1_g'], lw['ln1_b'], lw['w_up'], lw['b_up'], lw['w_down'],
               lw['b_down'], lw['ln2_g'], lw['ln2_b'])
    return pl.pallas_call(
        functools.partial(_out_mlp_kernel, alpha=alpha, ff_chunk=1024),
        grid=(t // tm,),
        in_specs=[tok(d), tok(W_A), tok(W_B), tok(W_C)] + [const(w) for w in weights],
        out_specs=tok(d),
        out_shape=jax.ShapeDtypeStruct((t, d), F32),
        compiler_params=pltpu.CompilerParams(dimension_semantics=("parallel",),
                                             vmem_limit_bytes=VMEM_LIMIT_BYTES),
        name="out_mlp",
    )(x2d, ya, yb, yc, *weights)


def _rope_tables(pos):
    half = ROPE_DIM // 2
    inv = ROPE_THETA ** (-jnp.arange(half, dtype=F32) / half)
    ang = pos.astype(F32)[:, None] * inv[None, :]
    cos, sin = jnp.cos(ang), jnp.sin(ang)
    cos_t = jnp.tile(jnp.concatenate([cos, cos], -1), (1, H_B))
    sin_t = jnp.tile(jnp.concatenate([-sin, sin], -1), (1, H_B))
    return cos_t, sin_t


def _strict_lower_ones(n):
    j = lax.broadcasted_iota(jnp.int32, (n, n), 0)
    s = lax.broadcasted_iota(jnp.int32, (n, n), 1)
    return (j > s).astype(BF16)


def _row(a):
    return a.reshape(1, -1).astype(F32)


def _prep_layer(l, w_in, w_s, b_s, g_cq, g_ckv, w_uq, w_uk, w_uv, g_mix, w_out, ln1_g, ln1_b,
                w_up, b_up, w_down, b_down, ln2_g, ln2_b, n_dec):
    wi = w_in[l].astype(BF16)
    o_cq, o_ckv, o_kr, o_c = 2 * W_A, 2 * W_A + Q_LORA, 2 * W_A + Q_LORA + KV_LORA, \
        2 * W_A + Q_LORA + KV_LORA + ROPE_DIM
    uq = w_uq[l].astype(BF16)
    lw = {
        'w_a': wi[:, :o_cq], 'w_ckv': wi[:, o_ckv:o_kr], 'w_c': wi[:, o_c:],
        'w_cqkr': jnp.concatenate([wi[:, o_cq:o_ckv], wi[:, o_kr:o_c], wi[:, o_kr:o_c]], axis=1),
        'g_cq': _row(g_cq[l]), 'g_ckv': _row(g_ckv[l]),
        'w_uqn': uq[:, :, :NOPE_DIM].reshape(Q_LORA, H_B * NOPE_DIM),
        'w_uqr': uq[:, :, NOPE_DIM:].reshape(Q_LORA, H_B * ROPE_DIM),
        'w_ukt': jnp.swapaxes(w_uk[l], 1, 2).astype(BF16),
        'w_uv': w_uv[l].astype(BF16),
        'g_a': _row(g_mix[l, :W_A]), 'g_b': _row(g_mix[l, W_A:W_A + W_B]),
        'g_c': _row(g_mix[l, W_A + W_B:]),
        'w_out': w_out[l].astype(BF16), 'ln1_g': _row(ln1_g[l]), 'ln1_b': _row(ln1_b[l]),
        'w_up': w_up[l].astype(BF16), 'b_up': _row(b_up[l]), 'w_down': w_down[l].astype(BF16),
        'b_down': _row(b_down[l]), 'ln2_g': _row(ln2_g[l]), 'ln2_b': _row(ln2_b[l]),
    }
    seg = lax.broadcasted_iota(jnp.int32, (W_A, W_A), 0) // DG_A
    lw['seg'] = (seg == seg.T).astype(BF16)
    idx = jnp.arange(GMLP_CHUNK)
    vis = (idx[None, :] // CHUNK) <= (idx[:, None] // CHUNK)
    wm = jnp.where(vis[None], w_s[l], 0.0)
    lw['mixw'] = jnp.transpose(wm, (1, 0, 2)).reshape(GMLP_CHUNK, G_A * GMLP_CHUNK).astype(BF16)
    lw['mixb'] = jnp.repeat(b_s[l].T, DG_A, axis=1).astype(F32)
    reps = GMLP_CHUNK // n_dec
    wd = w_s[l][:, :n_dec, :n_dec]
    wmd = jax.vmap(lambda w: jnp.kron(jnp.eye(reps, dtype=F32), w))(wd)
    lw_dec = dict(lw)
    lw_dec['mixw'] = jnp.transpose(wmd, (1, 0, 2)).reshape(GMLP_CHUNK, G_A * GMLP_CHUNK).astype(BF16)
    lw_dec['mixb'] = jnp.repeat(jnp.tile(b_s[l][:, :n_dec].T, (reps, 1)), DG_A, axis=1).astype(F32)
    return lw, lw_dec


def kernel(x_prompt, x_sample, cache_mla_ckv, cache_mla_krope, cache_sb_k, cache_sb_v, w_in, w_s, b_s, g_cq, g_ckv, w_uq, w_uk, w_uv, g_mix, w_out, ln1_g, ln1_b, w_up, b_up, w_down, b_down, ln2_g, ln2_b):
    b, s, d = x_prompt.shape
    db, ds, _ = x_sample.shape
    depth = w_in.shape[0]
    past = cache_mla_ckv.shape[2]
    alpha = (2 * depth) ** 0.25
    tm_p, tq = 512, 256
    t_dec = db * ds
    assert t_dec == GMLP_CHUNK and s % tm_p == 0 and past % CHUNK == 0 and ds <= CHUNK

    cos_p, sin_p = _rope_tables(jnp.arange(s, dtype=jnp.int32))
    cos_s, sin_s = _rope_tables(jnp.tile(past + jnp.arange(ds, dtype=jnp.int32), db))
    tri_p = _strict_lower_ones(tq)
    tk_dec = min(2048, past)
    assert past % tk_dec == 0
    tri_dp = _strict_lower_ones(256)
    tri_dn = _strict_lower_ones(LANES)
    cache_kt = cache_sb_k.transpose(0, 1, 3, 4, 2).reshape(depth, db, W_C, past)
    cache_vt = cache_sb_v.transpose(0, 1, 3, 4, 2).reshape(depth, db, W_C, past)
    cache_krt = cache_mla_krope.transpose(0, 1, 3, 2)

    xp = x_prompt.reshape(b * s, d)
    xs = x_sample.reshape(t_dec, d)
    st_p = [[] for _ in range(4)]
    st_s = [[] for _ in range(5)]
    for l in range(depth):
        lw, lw_dec = _prep_layer(l, w_in, w_s, b_s, g_cq, g_ckv, w_uq, w_uk, w_uv, g_mix, w_out,
                                 ln1_g, ln1_b, w_up, b_up, w_down, b_down, ln2_g, ln2_b, ds)
        ya, ql, qr, kcat, sbq, sbk, sbv, ckv, kr, kk, vv = _in_proj(
            xp, lw, cos_p, sin_p, tm=tm_p, want_gv=False)
        r3 = lambda a: a.reshape(b, s, a.shape[-1])
        yb = _mla_prompt(r3(ql), r3(qr), r3(kcat), lw['w_uv'], lw['g_b'], tq=tq)
        yc = _sb_prompt(r3(sbq), r3(sbk), r3(sbv), tri_p, lw['g_c'], tq=tq)
        xp = _out_mlp(xp, ya, yb.reshape(b * s, W_B), yc.reshape(b * s, W_C), lw, tm=tm_p, alpha=alpha)
        for lst, a in zip(st_p, (ckv, kr, kk, vv)):
            lst.append(a)
        ya, ql, qr, kcat, sbq, sbk, sbv, ckv, kr, kk, vv, gv = _in_proj(
            xs, lw_dec, cos_s, sin_s, tm=t_dec, want_gv=True)
        ql_s = ql.reshape(db, ds, H_B, KV_LORA).transpose(0, 2, 1, 3).reshape(db, H_B * ds, KV_LORA)
        qr_s = qr.reshape(db, ds, H_B, ROPE_DIM).transpose(0, 2, 1, 3).reshape(db, H_B * ds, ROPE_DIM)
        pad_new = lambda a: jnp.pad(a.reshape(db, ds, a.shape[-1]), ((0, 0), (0, LANES - ds), (0, 0)))
        yb = _mla_decode(ql_s, qr_s, cache_mla_ckv, cache_krt, l, pad_new(ckv), pad_new(kr),
                         lw['w_uv'], lw['g_b'], tk=tk_dec)
        yc = _sb_decode(sbq, cache_kt, cache_vt, l, pad_new(sbk), pad_new(sbv), tri_dp, tri_dn,
                        lw['g_c'], n_new=ds, tk=tk_dec)
        xs = _out_mlp(xs, ya, yb.reshape(t_dec, W_B), yc, lw, tm=t_dec, alpha=alpha)
        for lst, a in zip(st_s, (ckv, kr, kk, vv, gv)):
            lst.append(a)

    def stack_p(lst, tail):
        return jnp.stack(lst).reshape((depth, b, s) + tail)

    def stack_s(lst, tail):
        return jnp.stack(lst).reshape((depth, db, ds) + tail)

    return (xp.reshape(b, s, d), xs.reshape(db, ds, d),
            stack_p(st_p[0], (KV_LORA,)), stack_p(st_p[1], (ROPE_DIM,)),
            stack_p(st_p[2], (H_C, D_C)), stack_p(st_p[3], (H_C, D_C)),
            stack_s(st_s[0], (KV_LORA,)), stack_s(st_s[1], (ROPE_DIM,)),
            stack_s(st_s[2], (H_C, D_C)), stack_s(st_s[3], (H_C, D_C)),
            stack_s(st_s[4], (G_A, DG_A)))
```

```python
import functools

import jax
import jax.numpy as jnp
from jax import lax
from jax.experimental import pallas as pl
from jax.experimental.pallas import tpu as pltpu

F32 = jnp.float32
BF16 = jnp.bfloat16

CHUNK = 64
GMLP_CHUNK = 128
G_A, DG_A = 4, 64
W_A = G_A * DG_A
H_B, NOPE_DIM, ROPE_DIM, V_DIM = 4, 128, 64, 128
W_B = H_B * V_DIM
Q_LORA, KV_LORA = 384, 256
ROPE_THETA = 10000.0
LOG2E = 1.4426950408889634
MLA_SCALE = (NOPE_DIM + ROPE_DIM) ** -0.5
MLA_QSCALE = MLA_SCALE * LOG2E
H_C, D_C = 4, 64
W_C = H_C * D_C
SB_SCALE = D_C ** -0.5
SB_QSCALE = SB_SCALE * LOG2E
EPS = 1e-5
NEG_BIG = -1e30

LANES = 128
VMEM_LIMIT_BYTES = 56 * 1024 * 1024


def _dot(a, b):
    return jnp.dot(a, b, preferred_element_type=F32)


def _dot_nt(a, b):
    return lax.dot_general(a, b, (((1,), (1,)), ((), ())), preferred_element_type=F32)


def _rep(x, n):
    return x if n == 1 else jnp.concatenate([x] * n, axis=1)


def _rms(x, g):
    ms = jnp.mean(x * x, axis=-1, keepdims=True)
    return x * lax.rsqrt(ms + EPS) * g


def _layer_norm(x, g, b):
    mu = jnp.mean(x, axis=-1, keepdims=True)
    d = x - mu
    var = jnp.mean(d * d, axis=-1, keepdims=True)
    return d * lax.rsqrt(var + EPS) * g + b


def _gelu_tanh(x):
    c = 0.7978845608028654
    return 0.5 * x * (1.0 + jnp.tanh(c * (x + 0.044715 * (x * x * x))))


def _head_lane_id(width, per_head):
    return lax.broadcasted_iota(jnp.int32, (1, width), 1) // per_head


def _in_proj_kernel(x_ref, wa_ref, wcqkr_ref, wckv_ref, wc_ref, gcq_ref, gckv_ref,
                    wuqn_ref, wuqr_ref, wukt_ref, mixw_ref, mixb_ref, ga_ref, seg_ref,
                    cos_ref, sin_ref, *refs, tm, key_minor_state):
    if key_minor_state:
        refs = refs[4:]
    (ya_ref, ql_ref, qr_ref, kcat_ref, sbq_ref, sbk_ref, sbv_ref,
     ckv_ref, kr_ref, k_ref, v_ref, *gv_refs) = refs
    xb = x_ref[...].astype(BF16)
    grp = _head_lane_id(W_A, DG_A)
    lower_half = (lax.broadcasted_iota(jnp.int32, (1, 256), 1) % ROPE_DIM) < (ROPE_DIM // 2)
    cos = cos_ref[...]
    sin = sin_ref[...]

    def seg_mean(val):
        return _dot(val.astype(BF16), seg_ref[...]) * (1.0 / DG_A)

    def rope(val):
        w = val.shape[1]
        swapped = jnp.where(lower_half[:, :w], pltpu.roll(val, w - ROPE_DIM // 2, 1),
                            pltpu.roll(val, ROPE_DIM // 2, 1))
        return val * cos[:, :w] + swapped * sin[:, :w]

    pa = _dot(xb, wa_ref[...])
    u = _gelu_tanh(pa[:, :W_A])
    gv = _gelu_tanh(pa[:, W_A:])
    mu = seg_mean(gv)
    dv = gv - mu
    v = dv * lax.rsqrt(seg_mean(dv * dv) + EPS)
    if gv_refs:
        gv_refs[0][...] = v
    vb = v.astype(BF16)
    zero_b = jnp.zeros((GMLP_CHUNK, W_A), BF16)
    mixed = []
    for c in range(tm // GMLP_CHUNK):
        vc = vb[c * GMLP_CHUNK:(c + 1) * GMLP_CHUNK, :]
        stacked = jnp.concatenate([jnp.where(grp == g, vc, zero_b) for g in range(G_A)], axis=0)
        mixed.append(_dot(mixw_ref[...], stacked) + mixb_ref[...])
    y_a = u * jnp.concatenate(mixed, axis=0)
    ya_ref[...] = _rms(y_a, ga_ref[...]).astype(BF16)

    pq = _dot(xb, wcqkr_ref[...])
    cqn = _rms(pq[:, :Q_LORA], gcq_ref[...]).astype(BF16)
    qn = _dot(cqn, wuqn_ref[...])
    qr = rope(_dot(cqn, wuqr_ref[...]))
    qr_ref[...] = (qr * MLA_QSCALE).astype(BF16)
    for h in range(H_B):
        ql = _dot(qn[:, h * NOPE_DIM:(h + 1) * NOPE_DIM].astype(BF16), wukt_ref[h])
        ql_ref[:, h * KV_LORA:(h + 1) * KV_LORA] = (ql * MLA_QSCALE).astype(BF16)
    ckv = _rms(_dot(xb, wckv_ref[...]), gckv_ref[...])
    ckv_ref[...] = ckv
    kr2 = rope(pq[:, Q_LORA:])
    if key_minor_state:
        kr_ref[...] = kr2.T[:ROPE_DIM, :]
    else:
        kr_ref[...] = kr2[:, :ROPE_DIM]
    kcat_ref[:, :KV_LORA] = ckv.astype(BF16)
    kr2b = kr2.astype(BF16)
    kcat_ref[:, KV_LORA:KV_LORA + LANES] = kr2b
    kcat_ref[:, KV_LORA + LANES:] = kr2b

    pc = _dot(xb, wc_ref[...])
    sbq_ref[...] = (pc[:, :W_C] * SB_QSCALE).astype(BF16)
    kc = pc[:, W_C:2 * W_C]
    vc_ = pc[:, 2 * W_C:]
    if key_minor_state:
        k_ref[...] = kc.T
        v_ref[...] = vc_.T
    else:
        k_ref[...] = kc
        v_ref[...] = vc_
    sbk_ref[...] = kc.astype(BF16)
    sbv_ref[...] = vc_.astype(BF16)


def _in_proj(x2d, lw, cos_tab, sin_tab, *, tm, stacked_state=None, layer=None):
    t = x2d.shape[0]
    n_pos_blocks = cos_tab.shape[0] // tm
    tok = lambda w: pl.BlockSpec((tm, w), lambda i: (i, 0))
    full = lambda a: pl.BlockSpec(a.shape, lambda i: (0,) * a.ndim)
    pos = pl.BlockSpec((tm, 256), lambda i: (i % n_pos_blocks, 0))
    weights = (lw['w_a'], lw['w_cqkr'], lw['w_ckv'], lw['w_c'], lw['g_cq'], lw['g_ckv'],
               lw['w_uqn'], lw['w_uqr'], lw['w_ukt'], lw['mixw'], lw['mixb'], lw['g_a'], lw['seg'])
    out_shape = [
        jax.ShapeDtypeStruct((t, W_A), BF16),
        jax.ShapeDtypeStruct((t, H_B * KV_LORA), BF16),
        jax.ShapeDtypeStruct((t, H_B * ROPE_DIM), BF16),
        jax.ShapeDtypeStruct((t, 2 * KV_LORA), BF16),
        jax.ShapeDtypeStruct((t, W_C), BF16),
        jax.ShapeDtypeStruct((t, W_C), BF16),
        jax.ShapeDtypeStruct((t, W_C), BF16),
    ]
    out_specs = [tok(s.shape[1]) for s in out_shape]
    in_specs = [tok(x2d.shape[1])] + [full(w) for w in weights] + [pos, pos]
    operands = (x2d, *weights, cos_tab, sin_tab)
    aliases = {}
    if stacked_state is None:
        state = [jax.ShapeDtypeStruct((t, KV_LORA), F32), jax.ShapeDtypeStruct((t, ROPE_DIM), F32),
                 jax.ShapeDtypeStruct((t, W_C), F32), jax.ShapeDtypeStruct((t, W_C), F32),
                 jax.ShapeDtypeStruct((t, W_A), F32)]
        out_specs += [tok(s.shape[1]) for s in state]
    else:
        state = [jax.ShapeDtypeStruct(a.shape, a.dtype) for a in stacked_state]
        nb = stacked_state[2].shape[3] // tm
        key_minor = lambda rows: pl.BlockSpec((None, None, rows, tm),
                                              lambda i: (layer, i // nb, 0, i % nb))
        out_specs += [pl.BlockSpec((None, tm, KV_LORA), lambda i: (layer, i, 0)),
                      key_minor(ROPE_DIM), key_minor(W_C), key_minor(W_C)]
        in_specs += [pl.BlockSpec(memory_space=pl.ANY)] * len(stacked_state)
        aliases = {len(operands) + k: len(out_shape) + k for k in range(len(stacked_state))}
        operands += tuple(stacked_state)
    return pl.pallas_call(
        functools.partial(_in_proj_kernel, tm=tm, key_minor_state=stacked_state is not None),
        grid=(t // tm,),
        in_specs=in_specs,
        out_specs=out_specs,
        out_shape=out_shape + state,
        input_output_aliases=aliases,
        compiler_params=pltpu.CompilerParams(dimension_semantics=("parallel",),
                                             vmem_limit_bytes=VMEM_LIMIT_BYTES),
        name="in_proj",
    )(*operands)


def _mla_finish(acc, l_inv, wuv_ref, gb_ref, rows):
    out = acc * _rep(l_inv, KV_LORA // LANES)
    ys = [_dot(out[h * rows:(h + 1) * rows].astype(BF16), wuv_ref[h]) for h in range(H_B)]
    return _rms(jnp.concatenate(ys, axis=1), gb_ref[...]).astype(BF16)


def _mla_kernel(ql_ref, qr_ref, kcat_ref, hide_ref, wuv_ref, gb_ref, o_ref, qs_ref, sa_ref, sb_ref,
                m_ref, l_ref, acc_ref, *, tq):
    qi = pl.program_id(1)
    head = _head_lane_id(H_B * ROPE_DIM, ROPE_DIM)
    qr = qr_ref[0]
    for h in range(H_B):
        qs_ref[h * tq:(h + 1) * tq, :KV_LORA] = ql_ref[0, :, h * KV_LORA:(h + 1) * KV_LORA]
        qs_ref[h * tq:(h + 1) * tq, KV_LORA:] = jnp.where(head == h, qr, jnp.zeros_like(qr))

    m_ref[...] = jnp.full(m_ref.shape, NEG_BIG, F32)
    l_ref[...] = jnp.zeros(l_ref.shape, F32)
    acc_ref[...] = jnp.zeros(acc_ref.shape, F32)

    def key_block(j):
        return kcat_ref[0, pl.ds(pl.multiple_of(j * tq, tq), tq), :]

    def scores_into(dst_ref, j):
        dst_ref[...] = _dot_nt(qs_ref[...], key_block(j))

    def consume(src_ref, j, diagonal):
        s = src_ref[...]
        if diagonal:
            s = s + hide_ref[...]
        m_prev = m_ref[...]
        m_next = jnp.maximum(m_prev, jnp.max(s, axis=1, keepdims=True))
        p = jnp.exp2(s - _rep(m_next, tq // LANES))
        alpha = jnp.exp2(m_prev - m_next)
        l_ref[...] = alpha * l_ref[...] + jnp.sum(p, axis=1, keepdims=True)
        m_ref[...] = m_next
        acc_ref[...] = (acc_ref[...] * _rep(alpha, KV_LORA // LANES)
                        + _dot(p.astype(BF16), key_block(j)[:, :KV_LORA]))

    scores_into(sa_ref, 0)

    def pair(t, carry):
        scores_into(sb_ref, 2 * t + 1)
        consume(sa_ref, 2 * t, False)
        scores_into(sa_ref, 2 * t + 2)
        consume(sb_ref, 2 * t + 1, False)
        return carry

    lax.fori_loop(0, qi // 2, pair, 0)

    @pl.when(qi % 2 == 0)
    def _():
        consume(sa_ref, qi, True)

    @pl.when(qi % 2 == 1)
    def _():
        scores_into(sb_ref, qi)
        consume(sa_ref, qi - 1, False)
        consume(sb_ref, qi, True)

    o_ref[0] = _mla_finish(acc_ref[...], 1.0 / l_ref[...], wuv_ref, gb_ref, tq)


def _mla_prompt(ql, qr, kcat, w_uv, g_b, *, tq):
    b, s, _ = ql.shape
    rows = H_B * tq
    row = lax.broadcasted_iota(jnp.int32, (rows, tq), 0) % tq
    col = lax.broadcasted_iota(jnp.int32, (rows, tq), 1)
    hide = jnp.where((col // CHUNK) <= (row // CHUNK), 0.0, NEG_BIG).astype(F32)
    return pl.pallas_call(
        functools.partial(_mla_kernel, tq=tq),
        grid=(b, s // tq),
        in_specs=[
            pl.BlockSpec((1, tq, H_B * KV_LORA), lambda bi, qi: (bi, qi, 0)),
            pl.BlockSpec((1, tq, H_B * ROPE_DIM), lambda bi, qi: (bi, qi, 0)),
            pl.BlockSpec((1, s, 2 * KV_LORA), lambda bi, qi: (bi, 0, 0)),
            pl.BlockSpec(hide.shape, lambda bi, qi: (0, 0)),
            pl.BlockSpec(w_uv.shape, lambda bi, qi: (0, 0, 0)),
            pl.BlockSpec(g_b.shape, lambda bi, qi: (0, 0)),
        ],
        out_specs=pl.BlockSpec((1, tq, W_B), lambda bi, qi: (bi, qi, 0)),
        out_shape=jax.ShapeDtypeStruct((b, s, W_B), BF16),
        scratch_shapes=[pltpu.VMEM((rows, 2 * KV_LORA), BF16),
                        pltpu.VMEM((rows, tq), F32),
                        pltpu.VMEM((rows, tq), F32),
                        pltpu.VMEM((rows, LANES), F32),
                        pltpu.VMEM((rows, LANES), F32),
                        pltpu.VMEM((rows, KV_LORA), F32)],
        compiler_params=pltpu.CompilerParams(dimension_semantics=("parallel", "arbitrary"),
                                             vmem_limit_bytes=VMEM_LIMIT_BYTES),
        name="mla_prompt",
    )(ql, qr, kcat, hide, w_uv, g_b)


def _sb_block(z, vb, tri, carry, vis01=None, hide=None, v_is_transposed=False):
    neg_abs = lax.bitcast_convert_type(
        lax.bitcast_convert_type(z, jnp.uint32) | jnp.uint32(0x80000000), F32)
    skip = jnp.maximum(z, 0.0) + jnp.log(1.0 + jnp.exp2(neg_abs)) * LOG2E
    log_take = z - skip
    if vis01 is not None:
        skip = skip * vis01
    later = _dot(skip.astype(BF16), tri)
    arg = log_take - later - _rep(carry, z.shape[1] // LANES)
    if hide is not None:
        arg = arg + hide
    w = jnp.exp2(arg)
    pv = (_dot_nt if v_is_transposed else _dot)(w.astype(BF16), vb)
    return pv, carry + jnp.sum(skip, axis=1, keepdims=True)


def _strict_causal_masks(rows_per_head, n_rows, n_cols):
    row = lax.broadcasted_iota(jnp.int32, (n_rows, n_cols), 0) % rows_per_head
    col = lax.broadcasted_iota(jnp.int32, (n_rows, n_cols), 1)
    vis = col < row
    return jnp.where(vis, 1.0, 0.0).astype(F32), jnp.where(vis, 0.0, NEG_BIG).astype(F32)


def _stack_heads_masked(q, per_head):
    head = _head_lane_id(q.shape[1], per_head)
    return jnp.concatenate([jnp.where(head == h, q, jnp.zeros_like(q)) for h in range(H_C)], axis=0)


def _unstack_heads(acc, rows, per_head):
    head = _head_lane_id(acc.shape[1], per_head)
    y = jnp.zeros((rows, acc.shape[1]), F32)
    for h in range(H_C):
        y = y + jnp.where(head == h, acc[h * rows:(h + 1) * rows], 0.0)
    return y


def _sb_kernel(q_ref, k_ref, v_ref, tri_ref, vis_ref, hide_ref, gc_ref, o_ref, qs_ref, za_ref,
               zb_ref, c_ref, acc_ref, *, tq):
    qi = pl.program_id(1)
    qs_ref[...] = _stack_heads_masked(q_ref[0], D_C)

    def block_rows(j):
        return pl.ds(pl.multiple_of(j * tq, tq), tq)

    def logits_into(dst_ref, j):
        dst_ref[...] = _dot_nt(qs_ref[...], k_ref[0, block_rows(j), :])

    def consume(src_ref, j, diagonal=False):
        masks = (vis_ref[...], hide_ref[...]) if diagonal else ()
        carry_in = jnp.zeros(c_ref.shape, F32) if diagonal else c_ref[...]
        pv, carry = _sb_block(src_ref[...], v_ref[0, block_rows(j), :], tri_ref[...], carry_in,
                              *masks)
        acc_ref[...] = pv if diagonal else acc_ref[...] + pv
        c_ref[...] = carry

    logits_into(za_ref, qi)

    @pl.when(qi == 0)
    def _():
        consume(za_ref, qi, diagonal=True)

    @pl.when(qi > 0)
    def _():
        logits_into(zb_ref, qi - 1)
        consume(za_ref, qi, diagonal=True)

        def pair(t, carry):
            j = qi - 1 - 2 * t
            logits_into(za_ref, j - 1)
            consume(zb_ref, j)
            logits_into(zb_ref, j - 2)
            consume(za_ref, j - 1)
            return carry

        lax.fori_loop(0, (qi - 1) // 2, pair, 0)

        @pl.when(qi % 2 == 1)
        def _():
            consume(zb_ref, 0)

        @pl.when(qi % 2 == 0)
        def _():
            logits_into(za_ref, 0)
            consume(zb_ref, 1)
            consume(za_ref, 0)

    o_ref[0] = _rms(_unstack_heads(acc_ref[...], tq, D_C), gc_ref[...]).astype(BF16)


def _sb_prompt(q, k, v, tri, g_c, *, tq):
    b, s, _ = q.shape
    rows = H_C * tq
    vis01, hide = _strict_causal_masks(tq, rows, tq)
    return pl.pallas_call(
        functools.partial(_sb_kernel, tq=tq),
        grid=(b, s // tq),
        in_specs=[
            pl.BlockSpec((1, tq, W_C), lambda bi, qi: (bi, qi, 0)),
            pl.BlockSpec((1, s, W_C), lambda bi, qi: (bi, 0, 0)),
            pl.BlockSpec((1, s, W_C), lambda bi, qi: (bi, 0, 0)),
            pl.BlockSpec(tri.shape, lambda bi, qi: (0, 0)),
            pl.BlockSpec(vis01.shape, lambda bi, qi: (0, 0)),
            pl.BlockSpec(hide.shape, lambda bi, qi: (0, 0)),
            pl.BlockSpec(g_c.shape, lambda bi, qi: (0, 0)),
        ],
        out_specs=pl.BlockSpec((1, tq, W_C), lambda bi, qi: (bi, qi, 0)),
        out_shape=jax.ShapeDtypeStruct((b, s, W_C), BF16),
        scratch_shapes=[pltpu.VMEM((rows, W_C), BF16),
                        pltpu.VMEM((rows, tq), F32),
                        pltpu.VMEM((rows, tq), F32),
                        pltpu.VMEM((rows, LANES), F32),
                        pltpu.VMEM((rows, W_C), F32)],
        compiler_params=pltpu.CompilerParams(dimension_semantics=("parallel", "arbitrary"),
                                             vmem_limit_bytes=VMEM_LIMIT_BYTES),
        name="sb_prompt",
    )(q, k, v, tri, vis01, hide, g_c)


def _mla_dec_kernel(ql_ref, qr_ref, ckv_p_ref, krt_p_ref, ckv_n_ref, kr_n_ref, wuv_ref, gb_ref,
                    o_ref, m_ref, l_ref, acc_ref, *, n_new):
    kb_i = pl.program_id(1)

    @pl.when(kb_i == 0)
    def _():
        m_ref[...] = jnp.full(m_ref.shape, NEG_BIG, F32)
        l_ref[...] = jnp.zeros(l_ref.shape, F32)
        acc_ref[...] = jnp.zeros(acc_ref.shape, F32)

    def update(ckv_f32, rope_scores, n_valid):
        ckv_b = ckv_f32.astype(BF16)
        s = _dot_nt(ql_ref[0], ckv_b) + rope_scores
        if n_valid is not None:
            s = jnp.where(lax.broadcasted_iota(jnp.int32, s.shape, 1) < n_valid, s, NEG_BIG)
        m_prev = m_ref[...]
        m_next = jnp.maximum(m_prev, jnp.max(s, axis=1, keepdims=True))
        p = jnp.exp2(s - _rep(m_next, s.shape[1] // LANES))
        alpha = jnp.exp2(m_prev - m_next)
        l_ref[...] = alpha * l_ref[...] + jnp.sum(p, axis=1, keepdims=True)
        m_ref[...] = m_next
        acc_ref[...] = acc_ref[...] * _rep(alpha, KV_LORA // LANES) + _dot(p.astype(BF16), ckv_b)

    update(ckv_p_ref[...], _dot(qr_ref[0], krt_p_ref[...].astype(BF16)), None)

    @pl.when(kb_i == pl.num_programs(1) - 1)
    def _():
        update(ckv_n_ref[0], _dot_nt(qr_ref[0], kr_n_ref[0].astype(BF16)), n_new)
        o_ref[0] = _mla_finish(acc_ref[...], 1.0 / l_ref[...], wuv_ref, gb_ref, n_new)


def _mla_decode(ql_s, qr_s, cache_ckv, cache_kr, layer, ckv_new, kr_new, w_uv, g_b, *, tk):
    b, rows, _ = ql_s.shape
    n_new = rows // H_B
    past = cache_ckv.shape[2]
    return pl.pallas_call(
        functools.partial(_mla_dec_kernel, n_new=n_new),
        grid=(b, past // tk),
        in_specs=[
            pl.BlockSpec((1, rows, KV_LORA), lambda bi, ki: (bi, 0, 0)),
            pl.BlockSpec((1, rows, ROPE_DIM), lambda bi, ki: (bi, 0, 0)),
            pl.BlockSpec((None, None, tk, KV_LORA), lambda bi, ki: (layer, bi, ki, 0)),
            pl.BlockSpec((None, None, ROPE_DIM, tk), lambda bi, ki: (layer, bi, 0, ki)),
            pl.BlockSpec((1,) + ckv_new.shape[1:], lambda bi, ki: (bi, 0, 0)),
            pl.BlockSpec((1,) + kr_new.shape[1:], lambda bi, ki: (bi, 0, 0)),
            pl.BlockSpec(w_uv.shape, lambda bi, ki: (0, 0, 0)),
            pl.BlockSpec(g_b.shape, lambda bi, ki: (0, 0)),
        ],
        out_specs=pl.BlockSpec((1, n_new, W_B), lambda bi, ki: (bi, 0, 0)),
        out_shape=jax.ShapeDtypeStruct((b, n_new, W_B), BF16),
        scratch_shapes=[pltpu.VMEM((rows, LANES), F32),
                        pltpu.VMEM((rows, LANES), F32),
                        pltpu.VMEM((rows, KV_LORA), F32)],
        compiler_params=pltpu.CompilerParams(dimension_semantics=("parallel", "arbitrary"),
                                             vmem_limit_bytes=VMEM_LIMIT_BYTES),
        name="mla_decode",
    )(ql_s, qr_s, cache_ckv, cache_kr, ckv_new, kr_new, w_uv, g_b)


def _sb_dec_kernel(q_ref, kt_p_ref, vt_p_ref, k_n_ref, v_n_ref, tri_p_ref, tri_n_ref, gc_ref,
                   o_ref, qs_ref, c_ref, acc_ref, *, n_new):
    kb_i = pl.program_id(1)

    @pl.when(kb_i == 0)
    def _():
        qs_ref[...] = _stack_heads_masked(q_ref[...], D_C)
        vis01, hide = _strict_causal_masks(n_new, H_C * n_new, k_n_ref.shape[1])
        pv, carry = _sb_block(_dot_nt(qs_ref[...], k_n_ref[0]), v_n_ref[0], tri_n_ref[...],
                              jnp.zeros(c_ref.shape, F32), vis01, hide)
        acc_ref[...] = pv
        c_ref[...] = carry

    z = _dot(qs_ref[...], kt_p_ref[...].astype(BF16))
    vt = vt_p_ref[...].astype(BF16)
    sub = tri_p_ref.shape[0]
    acc = acc_ref[...]
    carry = c_ref[...]
    for c in reversed(range(z.shape[1] // sub)):
        cols = slice(c * sub, (c + 1) * sub)
        pv, carry = _sb_block(z[:, cols], vt[:, cols], tri_p_ref[...], carry, v_is_transposed=True)
        acc = acc + pv
    acc_ref[...] = acc
    c_ref[...] = carry

    @pl.when(kb_i == pl.num_programs(1) - 1)
    def _():
        o_ref[...] = _rms(_unstack_heads(acc_ref[...], n_new, D_C), gc_ref[...]).astype(BF16)


def _sb_decode(q, cache_k, cache_v, layer, k_new, v_new, tri_p, tri_n, g_c, *, n_new, tk):
    t = q.shape[0]
    b = t // n_new
    nkb = cache_k.shape[3] // tk
    rows = H_C * n_new
    return pl.pallas_call(
        functools.partial(_sb_dec_kernel, n_new=n_new),
        grid=(b, nkb),
        in_specs=[
            pl.BlockSpec((n_new, W_C), lambda bi, ki: (bi, 0)),
            pl.BlockSpec((None, None, W_C, tk), lambda bi, ki: (layer, bi, 0, nkb - 1 - ki)),
            pl.BlockSpec((None, None, W_C, tk), lambda bi, ki: (layer, bi, 0, nkb - 1 - ki)),
            pl.BlockSpec((1,) + k_new.shape[1:], lambda bi, ki: (bi, 0, 0)),
            pl.BlockSpec((1,) + v_new.shape[1:], lambda bi, ki: (bi, 0, 0)),
            pl.BlockSpec(tri_p.shape, lambda bi, ki: (0, 0)),
            pl.BlockSpec(tri_n.shape, lambda bi, ki: (0, 0)),
            pl.BlockSpec(g_c.shape, lambda bi, ki: (0, 0)),
        ],
        out_specs=pl.BlockSpec((n_new, W_C), lambda bi, ki: (bi, 0)),
        out_shape=jax.ShapeDtypeStruct((t, W_C), BF16),
        scratch_shapes=[pltpu.VMEM((rows, W_C), BF16),
                        pltpu.VMEM((rows, LANES), F32),
                        pltpu.VMEM((rows, W_C), F32)],
        compiler_params=pltpu.CompilerParams(dimension_semantics=("parallel", "arbitrary"),
                                             vmem_limit_bytes=VMEM_LIMIT_BYTES),
        name="sb_decode",
    )(q, cache_k, cache_v, k_new, v_new, tri_p, tri_n, g_c)


def _out_mlp_kernel(x_ref, ya_ref, yb_ref, yc_ref, wout_ref, g1_ref, b1_ref, wup_ref, bup_ref,
                    wdown_ref, bdown_ref, g2_ref, b2_ref, o_ref, *, alpha, ff_chunk):
    y = jnp.concatenate([ya_ref[...], yb_ref[...], yc_ref[...]], axis=1)
    x1 = _layer_norm(alpha * x_ref[...] + _dot(y, wout_ref[...]), g1_ref[...], b1_ref[...])
    x1b = x1.astype(BF16)
    h = jnp.zeros(x1.shape, F32)
    for c in range(wup_ref.shape[1] // ff_chunk):
        sl = slice(c * ff_chunk, (c + 1) * ff_chunk)
        a = jnp.maximum(_dot(x1b, wup_ref[:, sl]) + bup_ref[:, sl], 0.0)
        h = h + _dot((a * a).astype(BF16), wdown_ref[sl, :])
    o_ref[...] = _layer_norm(alpha * x1 + h + bdown_ref[...], g2_ref[...], b2_ref[...])


def _out_mlp(x2d, ya, yb, yc, lw, *, tm, alpha):
    t, d = x2d.shape
    tok = lambda w: pl.BlockSpec((tm, w), lambda i: (i, 0))
    const = lambda a: pl.BlockSpec(a.shape, lambda i: (0,) * a.ndim, pipeline_mode=pl.Buffered(1))
    weights = (lw['w_out'], lw['ln1_g'], lw['ln1_b'], lw['w_up'], lw['b_up'], lw['w_down'],
               lw['b_down'], lw['ln2_g'], lw['ln2_b'])
    return pl.pallas_call(
        functools.partial(_out_mlp_kernel, alpha=alpha, ff_chunk=1024),
        grid=(t // tm,),
        in_specs=[tok(d), tok(W_A), tok(W_B), tok(W_C)] + [const(w) for w in weights],
        out_specs=tok(d),
        out_shape=jax.ShapeDtypeStruct((t, d), F32),
        compiler_params=pltpu.CompilerParams(dimension_semantics=("parallel",),
                                             vmem_limit_bytes=VMEM_LIMIT_BYTES),
        name="out_mlp",
    )(x2d, ya, yb, yc, *weights)


def _rope_tables(pos):
    half = ROPE_DIM // 2
    inv = ROPE_THETA ** (-jnp.arange(half, dtype=F32) / half)
    ang = pos.astype(F32)[:, None] * inv[None, :]
    cos, sin = jnp.cos(ang), jnp.sin(ang)
    cos_t = jnp.tile(jnp.concatenate([cos, cos], -1), (1, H_B))
    sin_t = jnp.tile(jnp.concatenate([-sin, sin], -1), (1, H_B))
    return cos_t, sin_t


def _strict_lower_ones(n):
    j = lax.broadcasted_iota(jnp.int32, (n, n), 0)
    s = lax.broadcasted_iota(jnp.int32, (n, n), 1)
    return (j > s).astype(BF16)


def _row(a):
    return a.reshape(1, -1).astype(F32)


def _prep_layer(l, w_in, w_s, b_s, g_cq, g_ckv, w_uq, w_uk, w_uv, g_mix, w_out, ln1_g, ln1_b,
                w_up, b_up, w_down, b_down, ln2_g, ln2_b, n_dec):
    wi = w_in[l].astype(BF16)
    o_cq, o_ckv, o_kr, o_c = 2 * W_A, 2 * W_A + Q_LORA, 2 * W_A + Q_LORA + KV_LORA, \
        2 * W_A + Q_LORA + KV_LORA + ROPE_DIM
    uq = w_uq[l].astype(BF16)
    lw = {
        'w_a': wi[:, :o_cq], 'w_ckv': wi[:, o_ckv:o_kr], 'w_c': wi[:, o_c:],
        'w_cqkr': jnp.concatenate([wi[:, o_cq:o_ckv], wi[:, o_kr:o_c], wi[:, o_kr:o_c]], axis=1),
        'g_cq': _row(g_cq[l]), 'g_ckv': _row(g_ckv[l]),
        'w_uqn': uq[:, :, :NOPE_DIM].reshape(Q_LORA, H_B * NOPE_DIM),
        'w_uqr': uq[:, :, NOPE_DIM:].reshape(Q_LORA, H_B * ROPE_DIM),
        'w_ukt': jnp.swapaxes(w_uk[l], 1, 2).astype(BF16),
        'w_uv': w_uv[l].astype(BF16),
        'g_a': _row(g_mix[l, :W_A]), 'g_b': _row(g_mix[l, W_A:W_A + W_B]),
        'g_c': _row(g_mix[l, W_A + W_B:]),
        'w_out': w_out[l].astype(BF16), 'ln1_g': _row(ln1_g[l]), 'ln1_b': _row(ln1_b[l]),
        'w_up': w_up[l].astype(BF16), 'b_up': _row(b_up[l]), 'w_down': w_down[l].astype(BF16),
        'b_down': _row(b_down[l]), 'ln2_g': _row(ln2_g[l]), 'ln2_b': _row(ln2_b[l]),
    }
    seg = lax.broadcasted_iota(jnp.int32, (W_A, W_A), 0) // DG_A
    lw['seg'] = (seg == seg.T).astype(BF16)
    idx = jnp.arange(GMLP_CHUNK)
    vis = (idx[None, :] // CHUNK) <= (idx[:, None] // CHUNK)
    wm = jnp.where(vis[None], w_s[l], 0.0)
    lw['mixw'] = jnp.transpose(wm, (1, 0, 2)).reshape(GMLP_CHUNK, G_A * GMLP_CHUNK).astype(BF16)
    lw['mixb'] = jnp.repeat(b_s[l].T, DG_A, axis=1).astype(F32)
    reps = GMLP_CHUNK // n_dec
    wd = w_s[l][:, :n_dec, :n_dec]
    wmd = jax.vmap(lambda w: jnp.kron(jnp.eye(reps, dtype=F32), w))(wd)
    lw_dec = dict(lw)
    lw_dec['mixw'] = jnp.transpose(wmd, (1, 0, 2)).reshape(GMLP_CHUNK, G_A * GMLP_CHUNK).astype(BF16)
    lw_dec['mixb'] = jnp.repeat(jnp.tile(b_s[l][:, :n_dec].T, (reps, 1)), DG_A, axis=1).astype(F32)
    return lw, lw_dec


def kernel(x_prompt, x_sample, cache_mla_ckv, cache_mla_krope, cache_sb_k, cache_sb_v, w_in, w_s, b_s, g_cq, g_ckv, w_uq, w_uk, w_uv, g_mix, w_out, ln1_g, ln1_b, w_up, b_up, w_down, b_down, ln2_g, ln2_b):
    b, s, d = x_prompt.shape
    db, ds, _ = x_sample.shape
    depth = w_in.shape[0]
    past = cache_mla_ckv.shape[2]
    alpha = (2 * depth) ** 0.25
    tm_p, tq = 512, 256
    t_dec = db * ds
    assert t_dec == GMLP_CHUNK and s % tm_p == 0 and past % CHUNK == 0 and ds <= CHUNK

    cos_p, sin_p = _rope_tables(jnp.arange(s, dtype=jnp.int32))
    cos_s, sin_s = _rope_tables(jnp.tile(past + jnp.arange(ds, dtype=jnp.int32), db))
    tri_p = _strict_lower_ones(tq)
    tk_dec = min(2048, past)
    assert past % tk_dec == 0
    tri_dp = _strict_lower_ones(256)
    tri_dn = _strict_lower_ones(LANES)
    cache_kt = cache_sb_k.transpose(0, 1, 3, 4, 2).reshape(depth, db, W_C, past)
    cache_vt = cache_sb_v.transpose(0, 1, 3, 4, 2).reshape(depth, db, W_C, past)
    cache_krt = cache_mla_krope.transpose(0, 1, 3, 2)

    xp = x_prompt.reshape(b * s, d)
    xs = x_sample.reshape(t_dec, d)
    st_p = (jnp.zeros((depth, b * s, KV_LORA), F32), jnp.zeros((depth, b, ROPE_DIM, s), F32),
            jnp.zeros((depth, b, W_C, s), F32), jnp.zeros((depth, b, W_C, s), F32))
    st_s = [[] for _ in range(5)]
    for l in range(depth):
        lw, lw_dec = _prep_layer(l, w_in, w_s, b_s, g_cq, g_ckv, w_uq, w_uk, w_uv, g_mix, w_out,
                                 ln1_g, ln1_b, w_up, b_up, w_down, b_down, ln2_g, ln2_b, ds)
        ya, ql, qr, kcat, sbq, sbk, sbv, *st_p = _in_proj(
            xp, lw, cos_p, sin_p, tm=tm_p, stacked_state=st_p, layer=l)
        r3 = lambda a: a.reshape(b, s, a.shape[-1])
        yb = _mla_prompt(r3(ql), r3(qr), r3(kcat), lw['w_uv'], lw['g_b'], tq=tq)
        yc = _sb_prompt(r3(sbq), r3(sbk), r3(sbv), tri_p, lw['g_c'], tq=tq)
        xp = _out_mlp(xp, ya, yb.reshape(b * s, W_B), yc.reshape(b * s, W_C), lw, tm=tm_p, alpha=alpha)
        ya, ql, qr, kcat, sbq, sbk, sbv, ckv, kr, kk, vv, gv = _in_proj(
            xs, lw_dec, cos_s, sin_s, tm=t_dec)
        ql_s = ql.reshape(db, ds, H_B, KV_LORA).transpose(0, 2, 1, 3).reshape(db, H_B * ds, KV_LORA)
        qr_s = qr.reshape(db, ds, H_B, ROPE_DIM).transpose(0, 2, 1, 3).reshape(db, H_B * ds, ROPE_DIM)
        pad_new = lambda a: jnp.pad(a.reshape(db, ds, a.shape[-1]), ((0, 0), (0, LANES - ds), (0, 0)))
        yb = _mla_decode(ql_s, qr_s, cache_mla_ckv, cache_krt, l, pad_new(ckv), pad_new(kr),
                         lw['w_uv'], lw['g_b'], tk=tk_dec)
        yc = _sb_decode(sbq, cache_kt, cache_vt, l, pad_new(sbk), pad_new(sbv), tri_dp, tri_dn,
                        lw['g_c'], n_new=ds, tk=tk_dec)
        xs = _out_mlp(xs, ya, yb.reshape(t_dec, W_B), yc, lw, tm=t_dec, alpha=alpha)
        for lst, a in zip(st_s, (ckv, kr, kk, vv, gv)):
            lst.append(a)

    def stack_s(lst, tail):
        return jnp.stack(lst).reshape((depth, db, ds) + tail)

    def heads_last(a):
        return a.reshape(depth, b, H_C, D_C, s).transpose(0, 1, 4, 2, 3)

    return (xp.reshape(b, s, d), xs.reshape(db, ds, d),
            st_p[0].reshape(depth, b, s, KV_LORA), st_p[1].transpose(0, 1, 3, 2),
            heads_last(st_p[2]), heads_last(st_p[3]),
            stack_s(st_s[0], (KV_LORA,)), stack_s(st_s[1], (ROPE_DIM,)),
            stack_s(st_s[2], (H_C, D_C)), stack_s(st_s[3], (H_C, D_C)),
            stack_s(st_s[4], (G_A, DG_A)))
```
